```python
import math
import jax, jax.numpy as jnp
from jax import lax
import numpy as np

D_MODEL = 4096
BATCH = 1
SEQ = 8192
DEPTH = 2

HEAD_DIM = 128
N_HEADS_DIFF = D_MODEL // (2 * HEAD_DIM)
DIFF_QK_DIM = HEAD_DIM // 2
N_HEADS_FOX = D_MODEL // (2 * HEAD_DIM)
N_HEADS_NSA = D_MODEL // HEAD_DIM
N_KV_GROUPS_NSA = 4
CMP_BLOCK = 32
CMP_STRIDE = 16
SLC_BLOCK = 64
SLC_TOPK = 16
WINDOW = 512
CMP_HIDDEN = 256
D_FF = 4 * D_MODEL
ROPE_THETA = 10000.0
Q_BLOCK = 128
NSA_Q_BLOCK = 64
EPS = 1e-6
NEG = -1e30
BIG = 1e30
TINY = 1e-20
N_EVEN = (DEPTH + 1) // 2
N_ODD = DEPTH // 2
EVEN_IN = 6 * N_HEADS_DIFF * HEAD_DIM + N_HEADS_FOX
ODD_IN = N_HEADS_NSA * HEAD_DIM + 6 * N_KV_GROUPS_NSA * HEAD_DIM + 3 * N_HEADS_NSA

kernel_name = "hybrid_diff_fox_nsa_adaln_block"


def rms_norm(x, gain):
    xf = x.astype(jnp.float32)
    y = xf * lax.rsqrt(jnp.mean(xf * xf, axis=-1, keepdims=True) + EPS)
    return (y * gain.astype(jnp.float32)).astype(x.dtype)


def rope_tables(positions, dim):
    inv = ROPE_THETA ** (-jnp.arange(0, dim, 2, dtype=jnp.float32) / dim)
    ang = positions.astype(jnp.float32)[..., None] * inv
    return jnp.cos(ang), jnp.sin(ang)


def apply_rope(x, cos, sin):
    shp = cos.shape[:2] + (1,) * (x.ndim - 3) + cos.shape[-1:]
    c, s = cos.reshape(shp), sin.reshape(shp)
    x1, x2 = jnp.split(x.astype(jnp.float32), 2, axis=-1)
    return jnp.concatenate([x1 * c - x2 * s, x1 * s + x2 * c], axis=-1).astype(x.dtype)


def masked_softmax(s, mask):
    s = jnp.where(mask, s, NEG)
    e = jnp.exp(s - jnp.max(s, axis=-1, keepdims=True)) * mask
    return e / jnp.maximum(jnp.sum(e, axis=-1, keepdims=True), TINY)


def sweep_query_blocks(block_fn, seq_len, block):
    starts = jnp.arange(seq_len // block) * block
    out = lax.map(block_fn, starts)
    out = jnp.moveaxis(out, 0, 1)
    return out.reshape((out.shape[0], seq_len) + out.shape[3:])


def diff_attention(q, k, v, lam):
    S = q.shape[1]
    scale = DIFF_QK_DIM ** -0.5
    kpos = jnp.arange(S)

    def block(q0):
        qb = lax.dynamic_slice_in_dim(q, q0, Q_BLOCK, axis=1)
        s = jnp.einsum('bqhcd,bkhcd->bchqk', qb, k).astype(jnp.float32) * scale
        mask = (q0 + jnp.arange(Q_BLOCK))[:, None] >= kpos[None, :]
        p = jax.nn.softmax(jnp.where(mask, s, NEG), axis=-1)
        p = p[:, 0] - lam * p[:, 1]
        return jnp.einsum('bhqk,bkhd->bqhd', p.astype(v.dtype), v)

    return sweep_query_blocks(block, S, Q_BLOCK)


def forgetting_attention(q, k, v, log_f):
    S = q.shape[1]
    scale = HEAD_DIM ** -0.5
    F = jnp.transpose(jnp.cumsum(log_f, axis=1), (0, 2, 1))
    kpos = jnp.arange(S)

    def block(q0):
        qb = lax.dynamic_slice_in_dim(q, q0, Q_BLOCK, axis=1)
        Fq = lax.dynamic_slice_in_dim(F, q0, Q_BLOCK, axis=2)
        s = jnp.einsum('bqhd,bkhd->bhqk', qb, k).astype(jnp.float32) * scale
        s = s + Fq[..., None] - F[:, :, None, :]
        mask = (q0 + jnp.arange(Q_BLOCK))[:, None] >= kpos[None, :]
        p = jax.nn.softmax(jnp.where(mask, s, NEG), axis=-1)
        return jnp.einsum('bhqk,bkhd->bqhd', p.astype(v.dtype), v)

    return sweep_query_blocks(block, S, Q_BLOCK)


def compress(kv, pe, w1, w2):
    B, S, G, d = kv.shape
    halves = kv.reshape(B, S // CMP_STRIDE, CMP_STRIDE, G, d)
    blocks = jnp.concatenate([halves[:, :-1], halves[:, 1:]], axis=2)
    blocks = blocks + pe[None, None, :, None, :]
    flat = jnp.moveaxis(blocks, 3, 2).reshape(B, blocks.shape[1], G, CMP_BLOCK * d)
    return jax.nn.silu(flat @ w1) @ w2


def nsa_attention(q, kc, vc, ks, vs, kw, vw, gates):
    B, S, H, d = q.shape
    G = N_KV_GROUPS_NSA
    HG = H // G
    Tb = NSA_Q_BLOCK
    dt = q.dtype
    scale = d ** -0.5
    Nc = kc.shape[1]
    Ns = S // SLC_BLOCK
    n_sel = min(SLC_TOPK, Ns)
    cmp_start = jnp.arange(Nc) * CMP_STRIDE
    cmp_end = cmp_start + CMP_BLOCK - 1
    slc_start = jnp.arange(Ns) * SLC_BLOCK
    overlap = ((cmp_start[:, None] <= slc_start[None, :] + SLC_BLOCK - 1)
               & (cmp_end[:, None] >= slc_start[None, :])).astype(jnp.float32)
    ks_blocks = jnp.transpose(ks.reshape(B, Ns, SLC_BLOCK, G, d), (0, 3, 1, 2, 4))
    vs_blocks = jnp.transpose(vs.reshape(B, Ns, SLC_BLOCK, G, d), (0, 3, 1, 2, 4))
    pad = ((0, 0), (WINDOW, 0), (0, 0), (0, 0))
    kw_pad, vw_pad = jnp.pad(kw, pad), jnp.pad(vw, pad)
    bi = jnp.arange(B)[:, None, None, None]
    gi = jnp.arange(G)[None, :, None, None]
    jblk = jnp.arange(Ns)

    def block(q0):
        tq = q0 + jnp.arange(Tb)
        qg = lax.dynamic_slice_in_dim(q, q0, Tb, axis=1).reshape(B, Tb, G, HG, d)
        s_c = jnp.einsum('bqghd,bcgd->bghqc', qg, kc).astype(jnp.float32) * scale
        p_c = masked_softmax(s_c, cmp_end[None, :] <= tq[:, None])
        o_c = jnp.einsum('bghqc,bcgd->bqghd', p_c.astype(dt), vc)
        imp = jnp.einsum('bghqc,cj->bgqj', p_c, overlap)
        cur = (tq // SLC_BLOCK)[:, None]
        forced = (jblk[None, :] == 0) | (jblk[None, :] == cur) | (jblk[None, :] == cur - 1)
        imp = jnp.where(forced, BIG, imp)
        imp = jnp.where(jblk[None, :] > cur, NEG, imp)
        _, idx = lax.top_k(imp, n_sel)
        k_sel = ks_blocks[bi, gi, idx].reshape(B, G, Tb, n_sel * SLC_BLOCK, d)
        v_sel = vs_blocks[bi, gi, idx].reshape(B, G, Tb, n_sel * SLC_BLOCK, d)
        pos = (idx[..., None] * SLC_BLOCK + jnp.arange(SLC_BLOCK)).reshape(B, G, Tb, -1)
        m_s = (pos <= tq[None, None, :, None])[:, :, None]
        s_s = jnp.einsum('bqghd,bgqkd->bghqk', qg, k_sel).astype(jnp.float32) * scale
        p_s = jax.nn.softmax(jnp.where(m_s, s_s, NEG), axis=-1)
        o_s = jnp.einsum('bghqk,bgqkd->bqghd', p_s.astype(dt), v_sel)
        kwb = lax.dynamic_slice_in_dim(kw_pad, q0, WINDOW + Tb, axis=1)
        vwb = lax.dynamic_slice_in_dim(vw_pad, q0, WINDOW + Tb, axis=1)
        spos = q0 - WINDOW + jnp.arange(WINDOW + Tb)
        m_w = ((spos[None, :] <= tq[:, None]) & (spos[None, :] > tq[:, None] - WINDOW)
               & (spos[None, :] >= 0))
        s_w = jnp.einsum('bqghd,bkgd->bghqk', qg, kwb).astype(jnp.float32) * scale
        p_w = jax.nn.softmax(jnp.where(m_w, s_w, NEG), axis=-1)
        o_w = jnp.einsum('bghqk,bkgd->bqghd', p_w.astype(dt), vwb)
        gb = lax.dynamic_slice_in_dim(gates, q0, Tb, axis=1).reshape(B, Tb, G, HG, 3).astype(dt)
        o = gb[..., 0:1] * o_c + gb[..., 1:2] * o_s + gb[..., 2:3] * o_w
        return o.reshape(B, Tb, H, d)

    return sweep_query_blocks(block, S, Tb)


def even_mixer(h, w_in, b_forget, qk_gain_diff, qk_gain_fox, lam_params, subln, w_out,
               cos64, sin64, lam_init):
    B, S, _ = h.shape
    W = N_HEADS_DIFF * HEAD_DIM
    proj = h @ w_in
    qa, ka, va, qb, kb, vb, fb = jnp.split(proj, [W, 2 * W, 3 * W, 4 * W, 5 * W, 6 * W], axis=-1)
    qa = apply_rope(rms_norm(qa.reshape(B, S, N_HEADS_DIFF, 2, DIFF_QK_DIM), qk_gain_diff[0]), cos64, sin64)
    ka = apply_rope(rms_norm(ka.reshape(B, S, N_HEADS_DIFF, 2, DIFF_QK_DIM), qk_gain_diff[1]), cos64, sin64)
    va = va.reshape(B, S, N_HEADS_DIFF, HEAD_DIM)
    lp = lam_params.astype(jnp.float32)
    lam = jnp.exp(jnp.sum(lp[0] * lp[1])) - jnp.exp(jnp.sum(lp[2] * lp[3])) + lam_init
    oa = rms_norm(diff_attention(qa, ka, va, lam), subln) * (1.0 - lam_init)
    qb = rms_norm(qb.reshape(B, S, N_HEADS_FOX, HEAD_DIM), qk_gain_fox[0])
    kb = rms_norm(kb.reshape(B, S, N_HEADS_FOX, HEAD_DIM), qk_gain_fox[1])
    vb = vb.reshape(B, S, N_HEADS_FOX, HEAD_DIM)
    log_f = jax.nn.log_sigmoid(fb.astype(jnp.float32) + b_forget.astype(jnp.float32))
    ob = forgetting_attention(qb, kb, vb, log_f)
    o = jnp.concatenate([oa.reshape(B, S, -1), ob.reshape(B, S, -1)], axis=-1)
    return o @ w_out


def odd_mixer(h, w_in, q_gain, k_gain, cmp_pos, cmp_w1, cmp_w2, w_out, cos128, sin128):
    B, S, _ = h.shape
    H, G, d = N_HEADS_NSA, N_KV_GROUPS_NSA, HEAD_DIM
    proj = h @ w_in
    q, kc, vc, ks, vs, kw, vw, gl = jnp.split(
        proj, [H * d + i * G * d for i in range(7)], axis=-1)
    rs = lambda t: t.reshape(B, S, G, d)
    q = apply_rope(rms_norm(q.reshape(B, S, H, d), q_gain), cos128, sin128)
    kc = rms_norm(compress(apply_rope(rs(kc), cos128, sin128), cmp_pos[0], cmp_w1[0], cmp_w2[0]), k_gain[0])
    vc = compress(rs(vc), cmp_pos[1], cmp_w1[1], cmp_w2[1])
    ks = apply_rope(rms_norm(rs(ks), k_gain[1]), cos128, sin128)
    kw = apply_rope(rms_norm(rs(kw), k_gain[2]), cos128, sin128)
    gates = jax.nn.sigmoid(gl.astype(jnp.float32)).reshape(B, S, H, 3)
    o = nsa_attention(q, kc, vc, ks, rs(vs), kw, rs(vw), gates)
    return o.reshape(B, S, H * d) @ w_out


def setup_inputs(seed: int = 0) -> dict:
    key = jax.random.key(seed)
    ks = jax.random.split(key, 22)
    f32 = jnp.float32
    nrm = lambda k, shp, s: jax.random.normal(k, shp, f32) * s
    D = D_MODEL
    offset = jax.random.randint(ks[2], (BATCH, 1), 0, 1024, dtype=jnp.int32)
    positions = (jnp.arange(SEQ, dtype=jnp.int32)[None, :] + offset).astype(jnp.int32)
    return {
        "x": nrm(ks[0], (BATCH, SEQ, D), 1.0),
        "c": nrm(ks[1], (BATCH, D), 1.0),
        "positions": positions,
        "w_ada": nrm(ks[3], (DEPTH, D, 6 * D), 0.5 * D ** -0.5),
        "b_ada": nrm(ks[4], (DEPTH, 6 * D), 0.02),
        "norm_gain": 1.0 + nrm(ks[5], (DEPTH, 2, D), 0.02),
        "even_w_in": nrm(ks[6], (N_EVEN, D, EVEN_IN), D ** -0.5),
        "even_b_forget": 3.0 + nrm(ks[7], (N_EVEN, N_HEADS_FOX), 0.5),
        "even_qk_gain_diff": 1.0 + nrm(ks[8], (N_EVEN, 2, DIFF_QK_DIM), 0.02),
        "even_qk_gain_fox": 1.0 + nrm(ks[9], (N_EVEN, 2, HEAD_DIM), 0.02),
        "even_diff_lambda": nrm(ks[10], (N_EVEN, 4, DIFF_QK_DIM), 0.1),
        "even_diff_subln": 1.0 + nrm(ks[11], (N_EVEN, HEAD_DIM), 0.02),
        "even_w_out": nrm(ks[12], (N_EVEN, D, D), D ** -0.5),
        "odd_w_in": nrm(ks[13], (N_ODD, D, ODD_IN), D ** -0.5),
        "odd_q_gain": 1.0 + nrm(ks[14], (N_ODD, HEAD_DIM), 0.02),
        "odd_k_gain": 1.0 + nrm(ks[15], (N_ODD, 3, HEAD_DIM), 0.02),
        "odd_cmp_pos": nrm(ks[16], (N_ODD, 2, CMP_BLOCK, HEAD_DIM), 0.1),
        "odd_cmp_w1": nrm(ks[17], (N_ODD, 2, CMP_BLOCK * HEAD_DIM, CMP_HIDDEN), (CMP_BLOCK * HEAD_DIM) ** -0.5),
        "odd_cmp_w2": nrm(ks[18], (N_ODD, 2, CMP_HIDDEN, HEAD_DIM), CMP_HIDDEN ** -0.5),
        "odd_w_out": nrm(ks[19], (N_ODD, D, D), D ** -0.5),
        "mlp_w1": nrm(ks[20], (DEPTH, D, D_FF), D ** -0.5),
        "mlp_w2": nrm(ks[21], (DEPTH, D_FF, D), D_FF ** -0.5),
    }


def reference(x, c, positions, w_ada, b_ada, norm_gain, even_w_in, even_b_forget,
              even_qk_gain_diff, even_qk_gain_fox, even_diff_lambda, even_diff_subln,
              even_w_out, odd_w_in, odd_q_gain, odd_k_gain, odd_cmp_pos, odd_cmp_w1,
              odd_cmp_w2, odd_w_out, mlp_w1, mlp_w2):
    cos64, sin64 = rope_tables(positions, DIFF_QK_DIM)
    cos128, sin128 = rope_tables(positions, HEAD_DIM)
    cond = jax.nn.silu(c)
    for i in range(DEPTH):
        mod = (cond @ w_ada[i] + b_ada[i])[:, None, :]
        sh1, sc1, g1, sh2, sc2, g2 = jnp.split(mod, 6, axis=-1)
        h = rms_norm(x, norm_gain[i, 0]) * (1.0 + sc1) + sh1
        if i % 2 == 0:
            e = i // 2
            lam_init = 0.8 - 0.6 * math.exp(-0.3 * i)
            y = even_mixer(h, even_w_in[e], even_b_forget[e], even_qk_gain_diff[e],
                           even_qk_gain_fox[e], even_diff_lambda[e], even_diff_subln[e],
                           even_w_out[e], cos64, sin64, lam_init)
        else:
            o = i // 2
            y = odd_mixer(h, odd_w_in[o], odd_q_gain[o], odd_k_gain[o], odd_cmp_pos[o],
                          odd_cmp_w1[o], odd_cmp_w2[o], odd_w_out[o], cos128, sin128)
        x = x + g1 * y
        h = rms_norm(x, norm_gain[i, 1]) * (1.0 + sc2) + sh2
        x = x + g2 * (jnp.square(jax.nn.relu(h @ mlp_w1[i])) @ mlp_w2[i])
    return x
```

```python
import functools
import math

import numpy as np
import jax
import jax.numpy as jnp
from jax import lax
from jax.experimental import pallas as pl
from jax.experimental.pallas import tpu as pltpu

F32 = jnp.float32
BF16 = jnp.bfloat16

HEAD_DIM = 128
DIFF_QK_DIM = 64
N_KV_GROUPS = 4
CMP_BLOCK = 32
CMP_STRIDE = 16
SLC_BLOCK = 64
SLC_TOPK = 16
WINDOW = 512
ROPE_THETA = 10000.0
EPS = 1e-6
NEG = -1e30
BIG = 1e30
TINY = 1e-20
UNSELECTED_SCORE = 2.0 ** 99
LANES = 128
VMEM_LIMIT = 56 * 1024 * 1024

NT_DIMS = (((1,), (1,)), ((), ()))


def _params(sem):
    return pltpu.CompilerParams(dimension_semantics=sem, vmem_limit_bytes=VMEM_LIMIT)


def _adaln_kernel(cb_ref, w_ref, b_ref, o_ref):
    d_model, tn = w_ref.shape[1], w_ref.shape[2]
    rows_per_step = 256

    def body(r, acc):
        rows = pl.ds(pl.multiple_of(r * rows_per_step, rows_per_step), rows_per_step)
        cb = cb_ref[rows, :]
        cond = cb * jax.nn.sigmoid(cb)
        prod = w_ref[0, rows, :] * jnp.concatenate([cond] * (tn // LANES), axis=1)
        return acc + prod.reshape(rows_per_step // 8, 8, tn).sum(axis=0)

    acc = lax.fori_loop(0, d_model // rows_per_step, body, jnp.zeros((8, tn), F32))
    o_ref[0] = acc.sum(axis=0, keepdims=True) + b_ref[0]


def _adaln(c, w_ada, b_ada):
    depth, d_model, n_out = w_ada.shape
    tn = 512
    cb = jnp.broadcast_to(c.reshape(d_model, 1), (d_model, LANES))
    return pl.pallas_call(
        _adaln_kernel,
        grid=(depth, n_out // tn),
        in_specs=[
            pl.BlockSpec((d_model, LANES), lambda l, n: (0, 0)),
            pl.BlockSpec((1, d_model, tn), lambda l, n: (l, 0, n)),
            pl.BlockSpec((1, 1, tn), lambda l, n: (l, 0, n)),
        ],
        out_specs=pl.BlockSpec((1, 1, tn), lambda l, n: (l, 0, n)),
        out_shape=jax.ShapeDtypeStruct((depth, 1, n_out), F32),
        compiler_params=_params(("arbitrary", "arbitrary")),
    )(cb, w_ada, b_ada.reshape(depth, 1, n_out))


def _norm_mod_kernel(x_ref, g_ref, sc_ref, sh_ref, o_ref):
    x = x_ref[...]
    ms = jnp.mean(x * x, axis=-1, keepdims=True)
    y = x * lax.rsqrt(ms + EPS) * g_ref[...]
    o_ref[...] = (y * (1.0 + sc_ref[...]) + sh_ref[...]).astype(o_ref.dtype)


def _norm_mod(x, gain, scale, shift):
    s, d_model = x.shape
    ts = 256
    row = pl.BlockSpec((1, d_model), lambda i: (0, 0))
    return pl.pallas_call(
        _norm_mod_kernel,
        grid=(s // ts,),
        in_specs=[pl.BlockSpec((ts, d_model), lambda i: (i, 0)), row, row, row],
        out_specs=pl.BlockSpec((ts, d_model), lambda i: (i, 0)),
        out_shape=jax.ShapeDtypeStruct((s, d_model), BF16),
        compiler_params=_params(("arbitrary",)),
    )(x, gain.reshape(1, d_model), scale, shift)


def _mm_kernel(*refs, nk, epilogue):
    if epilogue == "resid":
        a_ref, b_ref, x_ref, g_ref, o_ref = refs[:5]
    else:
        a_ref, b_ref, o_ref = refs[:3]

    def finish(acc):
        if epilogue == "plain":
            o_ref[...] = acc.astype(o_ref.dtype)
        elif epilogue == "relu2":
            r = jnp.maximum(acc, 0.0)
            o_ref[...] = (r * r).astype(o_ref.dtype)
        else:
            o_ref[...] = x_ref[...] + g_ref[...] * acc

    part = jnp.dot(a_ref[...], b_ref[...], preferred_element_type=F32)
    if nk == 1:
        finish(part)
    else:
        acc_ref = refs[-1]
        k = pl.program_id(2)

        @pl.when(k == 0)
        def _():
            acc_ref[...] = part

        @pl.when(k > 0)
        def _():
            acc_ref[...] += part

        @pl.when(k == nk - 1)
        def _():
            finish(acc_ref[...])


def _matmul(a, b, *, bm, bn, bk, epilogue="plain", out_dtype=F32, resid=None, gate=None):
    m, kdim = a.shape
    n = b.shape[1]
    nk = kdim // bk
    in_specs = [pl.BlockSpec((bm, bk), lambda i, j, k: (i, k)),
                pl.BlockSpec((bk, bn), lambda i, j, k: (k, j))]
    args = [a, b]
    if epilogue == "resid":
        in_specs += [pl.BlockSpec((bm, bn), lambda i, j, k: (i, j)),
                     pl.BlockSpec((1, bn), lambda i, j, k: (0, j))]
        args += [resid, gate]
    scratch = [pltpu.VMEM((bm, bn), F32)] if nk > 1 else []
    return pl.pallas_call(
        functools.partial(_mm_kernel, nk=nk, epilogue=epilogue),
        grid=(m // bm, n // bn, nk),
        in_specs=in_specs,
        out_specs=pl.BlockSpec((bm, bn), lambda i, j, k: (i, j)),
        out_shape=jax.ShapeDtypeStruct((m, n), out_dtype),
        scratch_shapes=scratch,
        compiler_params=_params(("arbitrary", "arbitrary", "arbitrary")),
    )(*args)


def _rms(x, n_chunk, gain):
    sq = x * x
    if n_chunk == LANES:
        ss = jnp.sum(sq, axis=-1, keepdims=True)
    else:
        lane = lax.broadcasted_iota(jnp.int32, x.shape, 1)
        low = lane < n_chunk
        s_low = jnp.sum(jnp.where(low, sq, 0.0), axis=-1, keepdims=True)
        s_all = jnp.sum(sq, axis=-1, keepdims=True)
        ss = jnp.where(low, s_low, s_all - s_low)
    return x * lax.rsqrt(ss * (1.0 / n_chunk) + EPS) * gain


def _rope(x, cos_t, sin_t, n_chunk):
    half = n_chunk // 2
    if n_chunk == LANES:
        partner = pltpu.roll(x, half, 1)
    else:
        lane = lax.broadcasted_iota(jnp.int32, x.shape, 1)
        first = (lane % n_chunk) < half
        partner = jnp.where(first, pltpu.roll(x, LANES - half, 1), pltpu.roll(x, half, 1))
    return x * cos_t + partner * sin_t


def _split3(v):
    hi = v.astype(BF16)
    r1 = v - hi.astype(F32)
    mid = r1.astype(BF16)
    lo = (r1 - mid.astype(F32)).astype(BF16)
    return hi, mid, lo


def _even_prep_kernel(qa_ref, ka_ref, va_ref, qb_ref, kb_ref, vb_ref, fb_ref, cos_ref, sin_ref,
                      gdq_ref, gdk_ref, gfq_ref, gfk_ref, bf_ref,
                      qa_o, ka_o, va_o, qf_o, kf_o, vb_o, carry_ref, *, n_heads):
    i = pl.program_id(0)
    ts = qa_ref.shape[0]
    cos_t, sin_t = cos_ref[...], sin_ref[...]
    lane = lax.broadcasted_iota(jnp.int32, (ts, LANES), 1)
    ones_aug = jnp.where(lane < 3, 1.0, 0.0).astype(BF16)

    z = fb_ref[...] + bf_ref[...]
    logf = jnp.minimum(z, 0.0) - jnp.log(1.0 + jnp.exp(-jnp.abs(z)))
    r = lax.broadcasted_iota(jnp.int32, (ts, ts), 0)
    c = lax.broadcasted_iota(jnp.int32, (ts, ts), 1)
    tri = jnp.where(c <= r, 1.0, 0.0).astype(BF16)

    @pl.when(i == 0)
    def _():
        carry_ref[...] = jnp.zeros_like(carry_ref)

    f_cum = carry_ref[0:1, :]
    for part in _split3(logf):
        f_cum = f_cum + jnp.dot(tri, part, preferred_element_type=F32)
    carry_ref[...] = jnp.broadcast_to(f_cum[ts - 1:ts, :], carry_ref.shape)
    neg_hi, neg_mid, neg_lo = _split3(-f_cum)
    prow = lax.broadcasted_iota(jnp.int32, (LANES, LANES), 0)
    pcol = lax.broadcasted_iota(jnp.int32, (LANES, LANES), 1)

    for h in range(n_heads):
        sl = slice(h * LANES, (h + 1) * LANES)
        qa = _rope(_rms(qa_ref[:, sl], DIFF_QK_DIM, gdq_ref[...]), cos_t, sin_t, DIFF_QK_DIM)
        qa_o[:, sl] = (qa * (DIFF_QK_DIM ** -0.5)).astype(BF16)
        ka = _rope(_rms(ka_ref[:, sl], DIFF_QK_DIM, gdk_ref[...]), cos_t, sin_t, DIFF_QK_DIM)
        ka_o[:, sl] = ka.astype(BF16)
        qb = _rms(qb_ref[:, sl], HEAD_DIM, gfq_ref[...]) * (HEAD_DIM ** -0.5)
        qf_o[h, :, 0:LANES] = qb.astype(BF16)
        qf_o[h, :, LANES:2 * LANES] = ones_aug
        kb = _rms(kb_ref[:, sl], HEAD_DIM, gfk_ref[...])
        kf_o[h, :, 0:LANES] = kb.astype(BF16)
        aug = jnp.zeros((ts, LANES), F32)
        for lane_idx, part in enumerate((neg_hi, neg_mid, neg_lo)):
            place = jnp.where((prow == h) & (pcol == lane_idx), 1.0, 0.0).astype(BF16)
            aug = aug + jnp.dot(part, place, preferred_element_type=F32)
        kf_o[h, :, LANES:2 * LANES] = aug.astype(BF16)
    va_o[...] = va_ref[...].astype(BF16)
    vb_o[...] = vb_ref[...].astype(BF16)


def _even_prep(proj, cos_t, sin_t, gain_diff, gain_fox, b_forget, n_heads):
    s = proj.shape[0]
    ts = 256
    w = n_heads * HEAD_DIM
    seg = lambda k: pl.BlockSpec((ts, w), lambda i, k=k: (i, k))
    tab = pl.BlockSpec((ts, LANES), lambda i: (i, 0))
    row = pl.BlockSpec((1, LANES), lambda i: (0, 0))
    gdq = jnp.tile(gain_diff[0], 2).reshape(1, LANES)
    gdk = jnp.tile(gain_diff[1], 2).reshape(1, LANES)
    bfp = jnp.zeros((1, LANES), F32).at[0, :n_heads].set(b_forget)
    flat = jax.ShapeDtypeStruct((s, w), BF16)
    aug = jax.ShapeDtypeStruct((n_heads, s, 2 * LANES), BF16)
    return pl.pallas_call(
        functools.partial(_even_prep_kernel, n_heads=n_heads),
        grid=(s // ts,),
        in_specs=[seg(0), seg(1), seg(2), seg(3), seg(4), seg(5),
                  pl.BlockSpec((ts, LANES), lambda i: (i, 6 * w // LANES)),
                  tab, tab, row, row, row, row, row],
        out_specs=[pl.BlockSpec((ts, w), lambda i: (i, 0)),
                   pl.BlockSpec((ts, w), lambda i: (i, 0)),
                   pl.BlockSpec((ts, w), lambda i: (i, 0)),
                   pl.BlockSpec((n_heads, ts, 2 * LANES), lambda i: (0, i, 0)),
                   pl.BlockSpec((n_heads, ts, 2 * LANES), lambda i: (0, i, 0)),
                   pl.BlockSpec((ts, w), lambda i: (i, 0))],
        out_shape=[flat, flat, flat, aug, aug, flat],
        scratch_shapes=[pltpu.VMEM((8, LANES), F32)],
        compiler_params=_params(("arbitrary",)),
    )(proj, proj, proj, proj, proj, proj, proj, cos_t, sin_t, gdq, gdk,
      gain_fox[0].reshape(1, LANES), gain_fox[1].reshape(1, LANES), bfp)


def _flash_kernel(qi_ref, kj_ref, first_ref, last_ref, *refs, mode, tq, tk, nrep, lam_init):
    if mode == "diff":
        q_ref, k_ref, v_ref, lam_ref, sub_ref, o_ref, qs_ref, m_ref, l_ref, acc_ref = refs
    elif mode == "fox":
        q_ref, k_ref, v_ref, _, o_ref, m_ref, l_ref, acc_ref = refs
    elif mode == "slc":
        q_ref, sel_ref, k_ref, v_ref, o_ref, qs_ref, m_ref, l_ref, acc_ref = refs
    else:
        q_ref, k_ref, v_ref, o_ref, qs_ref, m_ref, l_ref, acc_ref = refs
    t = pl.program_id(1)
    i, j = qi_ref[t], kj_ref[t]
    rows = nrep * tq

    @pl.when(first_ref[t] == 1)
    def _():
        m_ref[...] = jnp.full(m_ref.shape, NEG, F32)
        l_ref[...] = jnp.zeros(l_ref.shape, F32)
        acc_ref[...] = jnp.zeros(acc_ref.shape, F32)
        if mode == "diff":
            q = q_ref[...]
            lane = lax.broadcasted_iota(jnp.int32, q.shape, 1)
            qs_ref[0:tq, :] = jnp.where(lane < DIFF_QK_DIM, q, jnp.zeros_like(q))
            qs_ref[tq:2 * tq, :] = jnp.where(lane >= DIFF_QK_DIM, q, jnp.zeros_like(q))
        elif mode in ("slc", "win"):
            for hh in range(nrep):
                qs_ref[hh * tq:(hh + 1) * tq, 0:LANES] = q_ref[:, hh * LANES:(hh + 1) * LANES]
                if mode == "slc":
                    qs_ref[hh * tq:(hh + 1) * tq, LANES:2 * LANES] = sel_ref[0]

    if mode == "fox":
        q_all, k_all, v_all = q_ref[0], k_ref[0], v_ref[...]
    elif mode == "slc":
        q_all, k_all, v_all = qs_ref[...], k_ref[0], v_ref[...]
    else:
        q_all, k_all, v_all = qs_ref[...], k_ref[...], v_ref[...]
    s = lax.dot_general(q_all, k_all, NT_DIMS, preferred_element_type=F32)

    def update(scores):
        m_prev = m_ref[...]
        m_new = jnp.maximum(m_prev, jnp.max(scores, axis=-1, keepdims=True))
        alpha = jnp.exp(m_prev - m_new)
        p = jnp.exp(scores - m_new)
        l_ref[...] = alpha * l_ref[...] + jnp.sum(p, axis=-1, keepdims=True)
        acc_ref[...] = alpha * acc_ref[...] + jnp.dot(p.astype(BF16), v_all, preferred_element_type=F32)
        m_ref[...] = m_new

    q_lo, k_lo = i * tq, j * tk
    needs_mask = k_lo + tk - 1 > q_lo
    if mode == "win":
        needs_mask = needs_mask | (k_lo <= q_lo + tq - 1 - WINDOW)

    @pl.when(needs_mask)
    def _():
        qpos = q_lo + lax.broadcasted_iota(jnp.int32, (rows, tk), 0) % tq
        kpos = k_lo + lax.broadcasted_iota(jnp.int32, (rows, tk), 1)
        ok = kpos <= qpos
        if mode == "win":
            ok = ok & (kpos > qpos - WINDOW)
        update(jnp.where(ok, s, NEG))

    @pl.when(jnp.logical_not(needs_mask))
    def _():
        update(s)

    @pl.when(last_ref[t] == 1)
    def _():
        o = acc_ref[...] / l_ref[...]
        if mode == "diff":
            lp = lam_ref[...]
            lam = (jnp.exp(jnp.sum(lp[0:1] * lp[1:2], axis=-1, keepdims=True))
                   - jnp.exp(jnp.sum(lp[2:3] * lp[3:4], axis=-1, keepdims=True)) + lam_init)
            od = o[0:tq] - lam * o[tq:2 * tq]
            ms = jnp.mean(od * od, axis=-1, keepdims=True)
            o_ref[...] = (od * lax.rsqrt(ms + EPS) * sub_ref[...] * (1.0 - lam_init)).astype(o_ref.dtype)
        elif mode == "fox":
            o_ref[...] = o.astype(o_ref.dtype)
        else:
            for hh in range(nrep):
                o_ref[:, hh * LANES:(hh + 1) * LANES] = o[hh * tq:(hh + 1) * tq].astype(o_ref.dtype)


def _schedule(s, tq, tk, window=None):
    qi, kj, first, last = [], [], [], []
    for i in range(s // tq):
        hi = (i * tq + tq - 1) // tk
        lo = 0 if window is None else max(0, (i * tq - (window - 1)) // tk)
        for j in range(lo, hi + 1):
            qi.append(i), kj.append(j), first.append(int(j == lo)), last.append(int(j == hi))
    return tuple(jnp.asarray(np.asarray(a, np.int32)) for a in (qi, kj, first, last))


def _flash_call(mode, sched, n_outer, in_specs, out_spec, out_shape, scratch, args, *, tq, tk, nrep,
                lam_init=0.0, aliases=None):
    nsteps = sched[0].shape[0]
    grid_spec = pltpu.PrefetchScalarGridSpec(
        num_scalar_prefetch=4, grid=(n_outer, nsteps), in_specs=in_specs, out_specs=out_spec,
        scratch_shapes=scratch)
    return pl.pallas_call(
        functools.partial(_flash_kernel, mode=mode, tq=tq, tk=tk, nrep=nrep, lam_init=lam_init),
        grid_spec=grid_spec, out_shape=out_shape,
        input_output_aliases=aliases or {},
        compiler_params=_params(("arbitrary", "arbitrary")),
    )(*sched, *args)


def _softmax_scratch(rows):
    return [pltpu.VMEM((rows, 1), F32), pltpu.VMEM((rows, 1), F32), pltpu.VMEM((rows, LANES), F32)]


def _diff_attention(qa, ka, va, lam_params, subln, lam_init, n_heads, d_out):
    s = qa.shape[0]
    tq, tk = 256, 512
    sched = _schedule(s, tq, tk)
    in_specs = [
        pl.BlockSpec((tq, LANES), lambda h, t, qi, kj, f, l: (qi[t], h)),
        pl.BlockSpec((tk, LANES), lambda h, t, qi, kj, f, l: (kj[t], h)),
        pl.BlockSpec((tk, LANES), lambda h, t, qi, kj, f, l: (kj[t], h)),
        pl.BlockSpec((4, DIFF_QK_DIM), lambda h, t, qi, kj, f, l: (0, 0)),
        pl.BlockSpec((1, LANES), lambda h, t, qi, kj, f, l: (0, 0)),
    ]
    out_spec = pl.BlockSpec((tq, LANES), lambda h, t, qi, kj, f, l: (qi[t], h))
    scratch = [pltpu.VMEM((2 * tq, LANES), BF16)] + _softmax_scratch(2 * tq)
    return _flash_call("diff", sched, n_heads, in_specs, out_spec,
                       jax.ShapeDtypeStruct((s, d_out), BF16), scratch,
                       (qa, ka, va, lam_params, subln.reshape(1, LANES)),
                       tq=tq, tk=tk, nrep=2, lam_init=lam_init)


def _fox_attention(qf, kf, vb, o_prev, n_heads, head_offset):
    s = vb.shape[0]
    tq, tk = 256, 512
    sched = _schedule(s, tq, tk)
    in_specs = [
        pl.BlockSpec((1, tq, 2 * LANES), lambda h, t, qi, kj, f, l: (h, qi[t], 0)),
        pl.BlockSpec((1, tk, 2 * LANES), lambda h, t, qi, kj, f, l: (h, kj[t], 0)),
        pl.BlockSpec((tk, LANES), lambda h, t, qi, kj, f, l: (kj[t], h)),
        pl.BlockSpec(memory_space=pl.ANY),
    ]
    out_spec = pl.BlockSpec((tq, LANES), lambda h, t, qi, kj, f, l: (qi[t], h + head_offset))
    return _flash_call("fox", sched, n_heads, in_specs, out_spec,
                       jax.ShapeDtypeStruct(o_prev.shape, o_prev.dtype), _softmax_scratch(tq),
                       (qf, kf, vb, o_prev), tq=tq, tk=tk, nrep=1, aliases={7: 0})


def _odd_prep_kernel(q_ref, kc_ref, vc_ref, ks_ref, vs_ref, kw_ref, vw_ref, gl_ref, cos_ref, sin_ref,
                     qg_ref, kg_ref, pe_ref,
                     qn_o, ktop_o, kbot_o, vtop_o, vbot_o, ksk_o, vs_o, kw_o, vw_o, gates_o,
                     *, n_heads, n_groups):
    i = pl.program_id(0)
    ts = q_ref.shape[0]
    cos_t, sin_t = cos_ref[...], sin_ref[...]
    for h in range(n_heads):
        sl = slice(h * LANES, (h + 1) * LANES)
        q = _rope(_rms(q_ref[:, sl], HEAD_DIM, qg_ref[...]), cos_t, sin_t, HEAD_DIM)
        qn_o[:, sl] = (q * (HEAD_DIM ** -0.5)).astype(BF16)
    key_block = (i * ts + lax.broadcasted_iota(jnp.int32, (ts, LANES), 0)) // SLC_BLOCK
    lane = lax.broadcasted_iota(jnp.int32, (ts, LANES), 1)
    block_onehot = jnp.where(lane == key_block, UNSELECTED_SCORE, 0.0).astype(BF16)
    for g in range(n_groups):
        sl = slice(g * LANES, (g + 1) * LANES)
        kc = _rope(kc_ref[:, sl], cos_t, sin_t, HEAD_DIM)
        ktop_o[:, sl] = (kc + pe_ref[0]).astype(BF16)
        kbot_o[:, sl] = (kc + pe_ref[1]).astype(BF16)
        vc = vc_ref[:, sl]
        vtop_o[:, sl] = (vc + pe_ref[2]).astype(BF16)
        vbot_o[:, sl] = (vc + pe_ref[3]).astype(BF16)
        ks = _rope(_rms(ks_ref[:, sl], HEAD_DIM, kg_ref[1:2, :]), cos_t, sin_t, HEAD_DIM)
        ksk_o[g, :, 0:LANES] = ks.astype(BF16)
        ksk_o[g, :, LANES:2 * LANES] = block_onehot
        kw = _rope(_rms(kw_ref[:, sl], HEAD_DIM, kg_ref[2:3, :]), cos_t, sin_t, HEAD_DIM)
        kw_o[:, sl] = kw.astype(BF16)
    vs_o[...] = vs_ref[...].astype(BF16)
    vw_o[...] = vw_ref[...].astype(BF16)
    gates_o[...] = jax.nn.sigmoid(gl_ref[...])


def _odd_prep(proj, cos_t, sin_t, q_gain, k_gain, cmp_pos, n_heads, n_groups):
    s = proj.shape[0]
    ts = 256
    wq, wg = n_heads * HEAD_DIM, n_groups * HEAD_DIM
    seg = lambda k: pl.BlockSpec((ts, wg), lambda i, k=k: (i, wq // wg + k))
    tab = pl.BlockSpec((ts, LANES), lambda i: (i, 0))
    grp = pl.BlockSpec((ts, wg), lambda i: (i, 0))
    pe_tiles = jnp.stack([jnp.tile(cmp_pos[kv, half * CMP_STRIDE:(half + 1) * CMP_STRIDE], (ts // CMP_STRIDE, 1))
                          for kv in range(2) for half in range(2)])
    g_bf = jax.ShapeDtypeStruct((s, wg), BF16)
    return pl.pallas_call(
        functools.partial(_odd_prep_kernel, n_heads=n_heads, n_groups=n_groups),
        grid=(s // ts,),
        in_specs=[pl.BlockSpec((ts, wq), lambda i: (i, 0)), seg(0), seg(1), seg(2), seg(3), seg(4), seg(5),
                  pl.BlockSpec((ts, LANES), lambda i: (i, (wq + 6 * wg) // LANES)),
                  tab, tab,
                  pl.BlockSpec((1, LANES), lambda i: (0, 0)),
                  pl.BlockSpec((3, LANES), lambda i: (0, 0)),
                  pl.BlockSpec((4, ts, LANES), lambda i: (0, 0, 0))],
        out_specs=[pl.BlockSpec((ts, wq), lambda i: (i, 0)), grp, grp, grp, grp,
                   pl.BlockSpec((n_groups, ts, 2 * LANES), lambda i: (0, i, 0)),
                   grp, grp, grp, tab],
        out_shape=[jax.ShapeDtypeStruct((s, wq), BF16), g_bf, g_bf, g_bf, g_bf,
                   jax.ShapeDtypeStruct((n_groups, s, 2 * LANES), BF16),
                   g_bf, g_bf, g_bf, jax.ShapeDtypeStruct((s, LANES), F32)],
        compiler_params=_params(("arbitrary",)),
    )(proj, proj, proj, proj, proj, proj, proj, proj, cos_t, sin_t,
      q_gain.reshape(1, LANES), k_gain, pe_tiles)


def _compress_kernel(top_ref, bot_ref, w1_ref, w2_ref, kg_ref, o_ref):
    kv = pl.program_id(0)
    n_half = top_ref.shape[2]
    k_half = top_ref.shape[3]
    h_top = jnp.dot(top_ref[0, 0], w1_ref[0, 0:k_half, :].astype(BF16), preferred_element_type=F32)
    h_bot = jnp.dot(bot_ref[0, 0], w1_ref[0, k_half:2 * k_half, :].astype(BF16), preferred_element_type=F32)
    hid = h_top + pltpu.roll(h_bot, n_half - 1, 0)
    act = hid * jax.nn.sigmoid(hid)
    out = jnp.dot(act.astype(BF16), w2_ref[0].astype(BF16), preferred_element_type=F32)
    ms = jnp.mean(out * out, axis=-1, keepdims=True)
    normed = out * lax.rsqrt(ms + EPS) * kg_ref[...]
    o_ref[0, 0] = jnp.where(kv == 0, normed, out).astype(o_ref.dtype)


def _compress(top, bot, w1, w2, k_gain0):
    _, n_groups, n_half, k_half = top.shape
    hidden = w1.shape[2]
    blk = pl.BlockSpec((1, 1, n_half, k_half), lambda kv, g: (kv, g, 0, 0))
    return pl.pallas_call(
        _compress_kernel,
        grid=(2, n_groups),
        in_specs=[blk, blk,
                  pl.BlockSpec((1, 2 * k_half, hidden), lambda kv, g: (kv, 0, 0)),
                  pl.BlockSpec((1, hidden, HEAD_DIM), lambda kv, g: (kv, 0, 0)),
                  pl.BlockSpec((1, LANES), lambda kv, g: (0, 0))],
        out_specs=pl.BlockSpec((1, 1, n_half, HEAD_DIM), lambda kv, g: (kv, g, 0, 0)),
        out_shape=jax.ShapeDtypeStruct((2, n_groups, n_half, HEAD_DIM), BF16),
        compiler_params=_params(("arbitrary", "arbitrary")),
    )(top, bot, w1, w2, k_gain0.reshape(1, LANES))


def _cmp_select_kernel(q_ref, kc_ref, vc_ref, ov_ref, oc_ref, sel_ref, *, tq, heads_per_group, n_blocks_pad):
    i = pl.program_id(1)
    kc, vc = kc_ref[0, 0], vc_ref[0, 0]
    n_cmp = kc.shape[0]
    tpos = i * tq + lax.broadcasted_iota(jnp.int32, (tq, n_cmp), 0)
    cidx = lax.broadcasted_iota(jnp.int32, (tq, n_cmp), 1)
    valid = cidx * CMP_STRIDE + (CMP_BLOCK - 1) <= tpos
    p_sum = jnp.zeros((tq, n_cmp), F32)
    for hh in range(heads_per_group):
        sl = slice(hh * LANES, (hh + 1) * LANES)
        s = lax.dot_general(q_ref[:, sl], kc, NT_DIMS, preferred_element_type=F32)
        s = jnp.where(valid, s, NEG)
        e = jnp.where(valid, jnp.exp(s - jnp.max(s, axis=-1, keepdims=True)), 0.0)
        p = e / jnp.maximum(jnp.sum(e, axis=-1, keepdims=True), TINY)
        oc_ref[:, sl] = jnp.dot(p.astype(BF16), vc, preferred_element_type=F32)
        p_sum = p_sum + p
    p_hi = p_sum.astype(BF16)
    p_lo = (p_sum - p_hi.astype(F32)).astype(BF16)
    imp = (lax.dot_general(ov_ref[...], p_hi, NT_DIMS, preferred_element_type=F32)
           + lax.dot_general(ov_ref[...], p_lo, NT_DIMS, preferred_element_type=F32))
    blk = lax.broadcasted_iota(jnp.int32, (n_blocks_pad, tq), 0).astype(F32)
    cur = ((i * tq + lax.broadcasted_iota(jnp.int32, (n_blocks_pad, tq), 1)) // SLC_BLOCK).astype(F32)
    forced = (blk == 0) | (blk == cur) | (blk == cur - 1)
    imp = jnp.where(forced, BIG, imp)
    imp = jnp.where(blk > cur, NEG, imp)
    sel = jnp.zeros((n_blocks_pad, tq), F32)
    for _ in range(SLC_TOPK):
        top = jnp.max(imp, axis=0, keepdims=True)
        first = jnp.min(jnp.where(imp == top, blk, float(n_blocks_pad)), axis=0, keepdims=True)
        hit = blk == first
        sel = jnp.where(hit, 1.0, sel)
        imp = jnp.where(hit, -jnp.inf, imp)
    sel_ref[0] = (sel.T - 1.0).astype(sel_ref.dtype)


def _cmp_select(qn, kcv, n_groups, heads_per_group):
    s = qn.shape[0]
    tq = 128
    n_cmp = kcv.shape[2]
    n_blocks_pad = LANES
    assert s // SLC_BLOCK <= n_blocks_pad
    cmp_start = np.arange(n_cmp) * CMP_STRIDE
    slc_start = np.arange(n_blocks_pad) * SLC_BLOCK
    overlap_t = ((cmp_start[None, :] <= slc_start[:, None] + SLC_BLOCK - 1)
                 & (cmp_start[None, :] + CMP_BLOCK - 1 >= slc_start[:, None]))
    overlap_t = jnp.asarray(overlap_t.astype(np.float32), dtype=BF16)
    wq = heads_per_group * HEAD_DIM
    return pl.pallas_call(
        functools.partial(_cmp_select_kernel, tq=tq, heads_per_group=heads_per_group, n_blocks_pad=n_blocks_pad),
        grid=(n_groups, s // tq),
        in_specs=[pl.BlockSpec((tq, wq), lambda g, i: (i, g)),
                  pl.BlockSpec((1, 1, n_cmp, HEAD_DIM), lambda g, i: (0, g, 0, 0)),
                  pl.BlockSpec((1, 1, n_cmp, HEAD_DIM), lambda g, i: (1, g, 0, 0)),
                  pl.BlockSpec((n_blocks_pad, n_cmp), lambda g, i: (0, 0))],
        out_specs=[pl.BlockSpec((tq, wq), lambda g, i: (i, g)),
                   pl.BlockSpec((1, tq, n_blocks_pad), lambda g, i: (g, i, 0))],
        out_shape=[jax.ShapeDtypeStruct(qn.shape, F32),
                   jax.ShapeDtypeStruct((n_groups, s, n_blocks_pad), BF16)],
        compiler_params=_params(("arbitrary", "arbitrary")),
    )(qn, kcv, kcv, overlap_t)


def _slc_attention(qn, sel, ksk, vs, n_groups, heads_per_group):
    s = qn.shape[0]
    tq, tk = 128, 512
    sched = _schedule(s, tq, tk)
    wq = heads_per_group * HEAD_DIM
    in_specs = [
        pl.BlockSpec((tq, wq), lambda g, t, qi, kj, f, l: (qi[t], g)),
        pl.BlockSpec((1, tq, LANES), lambda g, t, qi, kj, f, l: (g, qi[t], 0)),
        pl.BlockSpec((1, tk, 2 * LANES), lambda g, t, qi, kj, f, l: (g, kj[t], 0)),
        pl.BlockSpec((tk, LANES), lambda g, t, qi, kj, f, l: (kj[t], g)),
    ]
    out_spec = pl.BlockSpec((tq, wq), lambda g, t, qi, kj, f, l: (qi[t], g))
    rows = heads_per_group * tq
    scratch = [pltpu.VMEM((rows, 2 * LANES), BF16)] + _softmax_scratch(rows)
    return _flash_call("slc", sched, n_groups, in_specs, out_spec, jax.ShapeDtypeStruct(qn.shape, F32),
                       scratch, (qn, sel, ksk, vs), tq=tq, tk=tk, nrep=heads_per_group)


def _win_attention(qn, kw, vw, n_groups, heads_per_group):
    s = qn.shape[0]
    tq, tk = 128, 256
    sched = _schedule(s, tq, tk, window=WINDOW)
    wq = heads_per_group * HEAD_DIM
    in_specs = [
        pl.BlockSpec((tq, wq), lambda g, t, qi, kj, f, l: (qi[t], g)),
        pl.BlockSpec((tk, LANES), lambda g, t, qi, kj, f, l: (kj[t], g)),
        pl.BlockSpec((tk, LANES), lambda g, t, qi, kj, f, l: (kj[t], g)),
    ]
    out_spec = pl.BlockSpec((tq, wq), lambda g, t, qi, kj, f, l: (qi[t], g))
    rows = heads_per_group * tq
    scratch = [pltpu.VMEM((rows, LANES), BF16)] + _softmax_scratch(rows)
    return _flash_call("win", sched, n_groups, in_specs, out_spec, jax.ShapeDtypeStruct(qn.shape, F32),
                       scratch, (qn, kw, vw), tq=tq, tk=tk, nrep=heads_per_group)


def _combine_kernel(oc_ref, os_ref, ow_ref, g_ref, o_ref, *, n_heads):
    g = g_ref[...]
    for h in range(n_heads):
        sl = slice(h * LANES, (h + 1) * LANES)
        o = (g[:, 3 * h:3 * h + 1] * oc_ref[:, sl] + g[:, 3 * h + 1:3 * h + 2] * os_ref[:, sl]
             + g[:, 3 * h + 2:3 * h + 3] * ow_ref[:, sl])
        o_ref[:, sl] = o.astype(o_ref.dtype)


def _combine(oc, osel, ow, gates, n_heads):
    s, w = oc.shape
    ts = 256
    big = pl.BlockSpec((ts, w), lambda i: (i, 0))
    return pl.pallas_call(
        functools.partial(_combine_kernel, n_heads=n_heads),
        grid=(s // ts,),
        in_specs=[big, big, big, pl.BlockSpec((ts, LANES), lambda i: (i, 0))],
        out_specs=big,
        out_shape=jax.ShapeDtypeStruct((s, w), BF16),
        compiler_params=_params(("arbitrary",)),
    )(oc, osel, ow, gates)


def _rope_tables(positions, dim):
    inv = ROPE_THETA ** (-jnp.arange(0, dim, 2, dtype=F32) / dim)
    ang = positions.astype(F32)[:, None] * inv
    cos, sin = jnp.cos(ang), jnp.sin(ang)
    reps = LANES // dim
    cos_t = jnp.tile(jnp.concatenate([cos, cos], axis=-1), (1, reps))
    sin_t = jnp.tile(jnp.concatenate([-sin, sin], axis=-1), (1, reps))
    return cos_t, sin_t


def _pad_cols(w, n):
    return jnp.pad(w, ((0, 0), (0, n - w.shape[1])))


def _even_mixer(h, w_in, b_forget, gain_diff, gain_fox, lam_params, subln, tables, lam_init):
    d_model = h.shape[1]
    n_heads = d_model // (2 * HEAD_DIM)
    n_in = 6 * n_heads * HEAD_DIM + 1024
    proj = _matmul(h, _pad_cols(w_in, n_in).astype(BF16), bm=1024, bn=1024, bk=d_model)
    qa, ka, va, qf, kf, vb = _even_prep(proj, *tables, gain_diff, gain_fox, b_forget, n_heads)
    o = _diff_attention(qa, ka, va, lam_params, subln, lam_init, n_heads, d_model)
    return _fox_attention(qf, kf, vb, o, n_heads, n_heads)


def _odd_mixer(h, w_in, q_gain, k_gain, cmp_pos, cmp_w1, cmp_w2, tables):
    s, d_model = h.shape
    n_heads, n_groups = d_model // HEAD_DIM, N_KV_GROUPS
    n_in = d_model + 7 * n_groups * HEAD_DIM
    proj = _matmul(h, _pad_cols(w_in, n_in).astype(BF16), bm=1024, bn=n_in // 10, bk=d_model)
    (qn, ktop, kbot, vtop, vbot, ksk, vs, kw, vw, gates) = _odd_prep(
        proj, *tables, q_gain, k_gain, cmp_pos, n_heads, n_groups)

    def halves(k_part, v_part):
        a = jnp.stack([k_part, v_part]).reshape(2, s // CMP_STRIDE, CMP_STRIDE, n_groups, HEAD_DIM)
        return jnp.transpose(a, (0, 3, 1, 2, 4)).reshape(2, n_groups, s // CMP_STRIDE, CMP_STRIDE * HEAD_DIM)

    kcv = _compress(halves(ktop, vtop), halves(kbot, vbot), cmp_w1, cmp_w2, k_gain[0])
    hpg = n_heads // n_groups
    oc, sel = _cmp_select(qn, kcv, n_groups, hpg)
    osel = _slc_attention(qn, sel, ksk, vs, n_groups, hpg)
    ow = _win_attention(qn, kw, vw, n_groups, hpg)
    return _combine(oc, osel, ow, gates, n_heads)


def kernel(x, c, positions, w_ada, b_ada, norm_gain, even_w_in, even_b_forget, even_qk_gain_diff,
           even_qk_gain_fox, even_diff_lambda, even_diff_subln, even_w_out, odd_w_in, odd_q_gain,
           odd_k_gain, odd_cmp_pos, odd_cmp_w1, odd_cmp_w2, odd_w_out, mlp_w1, mlp_w2):
    batch, s, d_model = x.shape
    assert batch == 1
    depth = w_ada.shape[0]
    xs = x[0]
    pos = positions[0]
    tables64 = _rope_tables(pos, DIFF_QK_DIM)
    tables128 = _rope_tables(pos, HEAD_DIM)
    mod = _adaln(c, w_ada, b_ada)
    for i in range(depth):
        sh1, sc1, g1, sh2, sc2, g2 = [mod[i, :, k * d_model:(k + 1) * d_model] for k in range(6)]
        h = _norm_mod(xs, norm_gain[i, 0], sc1, sh1)
        if i % 2 == 0:
            e = i // 2
            lam_init = 0.8 - 0.6 * math.exp(-0.3 * i)
            o = _even_mixer(h, even_w_in[e], even_b_forget[e], even_qk_gain_diff[e], even_qk_gain_fox[e],
                            even_diff_lambda[e], even_diff_subln[e], tables64, lam_init)
            w_out = even_w_out[e]
        else:
            od = i // 2
            o = _odd_mixer(h, odd_w_in[od], odd_q_gain[od], odd_k_gain[od], odd_cmp_pos[od],
                           odd_cmp_w1[od], odd_cmp_w2[od], tables128)
            w_out = odd_w_out[od]
        xs = _matmul(o, w_out.astype(BF16), bm=512, bn=1024, bk=d_model, epilogue="resid", resid=xs, gate=g1)
        h = _norm_mod(xs, norm_gain[i, 1], sc2, sh2)
        ff = _matmul(h, mlp_w1[i].astype(BF16), bm=1024, bn=1024, bk=d_model, epilogue="relu2", out_dtype=BF16)
        xs = _matmul(ff, mlp_w2[i].astype(BF16), bm=1024, bn=1024, bk=2048, epilogue="resid", resid=xs, gate=g2)
    return xs[None]
```

```python
import functools
import math

import numpy as np
import jax
import jax.numpy as jnp
from jax import lax
from jax.experimental import pallas as pl
from jax.experimental.pallas import tpu as pltpu

F32 = jnp.float32
BF16 = jnp.bfloat16

HEAD_DIM = 128
DIFF_QK_DIM = 64
N_KV_GROUPS = 4
CMP_BLOCK = 32
CMP_STRIDE = 16
SLC_BLOCK = 64
SLC_TOPK = 16
WINDOW = 512
ROPE_THETA = 10000.0
EPS = 1e-6
NEG = -1e30
BIG = 1e30
TINY = 1e-20
UNSELECTED_SCORE = 2.0 ** 99
LOG2E = 1.4426950408889634
LANES = 128
VMEM_LIMIT = 56 * 1024 * 1024

NT_DIMS = (((1,), (1,)), ((), ()))


def _params(sem):
    return pltpu.CompilerParams(dimension_semantics=sem, vmem_limit_bytes=VMEM_LIMIT)


def _adaln_kernel(cb_ref, w_ref, b_ref, o_ref):
    d_model, tn = w_ref.shape[1], w_ref.shape[2]
    rows_per_step = 256

    def body(r, acc):
        rows = pl.ds(pl.multiple_of(r * rows_per_step, rows_per_step), rows_per_step)
        cb = cb_ref[rows, :]
        cond = cb * jax.nn.sigmoid(cb)
        prod = w_ref[0, rows, :] * jnp.concatenate([cond] * (tn // LANES), axis=1)
        return acc + prod.reshape(rows_per_step // 8, 8, tn).sum(axis=0)

    acc = lax.fori_loop(0, d_model // rows_per_step, body, jnp.zeros((8, tn), F32))
    o_ref[0] = acc.sum(axis=0, keepdims=True) + b_ref[0]


def _adaln(c, w_ada, b_ada):
    depth, d_model, n_out = w_ada.shape
    tn = 512
    cb = jnp.broadcast_to(c.reshape(d_model, 1), (d_model, LANES))
    return pl.pallas_call(
        _adaln_kernel,
        grid=(depth, n_out // tn),
        in_specs=[
            pl.BlockSpec((d_model, LANES), lambda l, n: (0, 0)),
            pl.BlockSpec((1, d_model, tn), lambda l, n: (l, 0, n)),
            pl.BlockSpec((1, 1, tn), lambda l, n: (l, 0, n)),
        ],
        out_specs=pl.BlockSpec((1, 1, tn), lambda l, n: (l, 0, n)),
        out_shape=jax.ShapeDtypeStruct((depth, 1, n_out), F32),
        compiler_params=_params(("arbitrary", "arbitrary")),
    )(cb, w_ada, b_ada.reshape(depth, 1, n_out))


def _norm_mod_kernel(x_ref, g_ref, sc_ref, sh_ref, o_ref):
    x = x_ref[...]
    ms = jnp.mean(x * x, axis=-1, keepdims=True)
    y = x * lax.rsqrt(ms + EPS) * g_ref[...]
    o_ref[...] = (y * (1.0 + sc_ref[...]) + sh_ref[...]).astype(o_ref.dtype)


def _norm_mod(x, gain, scale, shift):
    s, d_model = x.shape
    ts = 256
    row = pl.BlockSpec((1, d_model), lambda i: (0, 0))
    return pl.pallas_call(
        _norm_mod_kernel,
        grid=(s // ts,),
        in_specs=[pl.BlockSpec((ts, d_model), lambda i: (i, 0)), row, row, row],
        out_specs=pl.BlockSpec((ts, d_model), lambda i: (i, 0)),
        out_shape=jax.ShapeDtypeStruct((s, d_model), BF16),
        compiler_params=_params(("arbitrary",)),
    )(x, gain.reshape(1, d_model), scale, shift)


def _mm_kernel(*refs, nk, epilogue):
    if epilogue == "resid":
        a_ref, b_ref, x_ref, g_ref, o_ref = refs[:5]
    else:
        a_ref, b_ref, o_ref = refs[:3]

    def finish(acc):
        if epilogue == "plain":
            o_ref[...] = acc.astype(o_ref.dtype)
        elif epilogue == "relu2":
            r = jnp.maximum(acc, 0.0)
            o_ref[...] = (r * r).astype(o_ref.dtype)
        else:
            o_ref[...] = x_ref[...] + g_ref[...] * acc

    part = jnp.dot(a_ref[...], b_ref[...], preferred_element_type=F32)
    if nk == 1:
        finish(part)
    else:
        acc_ref = refs[-1]
        k = pl.program_id(2)

        @pl.when(k == 0)
        def _():
            acc_ref[...] = part

        @pl.when(k > 0)
        def _():
            acc_ref[...] += part

        @pl.when(k == nk - 1)
        def _():
            finish(acc_ref[...])


def _matmul(a, b, *, bm, bn, bk, epilogue="plain", out_dtype=F32, resid=None, gate=None):
    m, kdim = a.shape
    n = b.shape[1]
    nk = kdim // bk
    in_specs = [pl.BlockSpec((bm, bk), lambda i, j, k: (i, k)),
                pl.BlockSpec((bk, bn), lambda i, j, k: (k, j))]
    args = [a, b]
    if epilogue == "resid":
        in_specs += [pl.BlockSpec((bm, bn), lambda i, j, k: (i, j)),
                     pl.BlockSpec((1, bn), lambda i, j, k: (0, j))]
        args += [resid, gate]
    scratch = [pltpu.VMEM((bm, bn), F32)] if nk > 1 else []
    return pl.pallas_call(
        functools.partial(_mm_kernel, nk=nk, epilogue=epilogue),
        grid=(m // bm, n // bn, nk),
        in_specs=in_specs,
        out_specs=pl.BlockSpec((bm, bn), lambda i, j, k: (i, j)),
        out_shape=jax.ShapeDtypeStruct((m, n), out_dtype),
        scratch_shapes=scratch,
        compiler_params=_params(("arbitrary", "arbitrary", "arbitrary")),
    )(*args)


def _rms(x, n_chunk, gain):
    sq = x * x
    if n_chunk == LANES:
        ss = jnp.sum(sq, axis=-1, keepdims=True)
    else:
        lane = lax.broadcasted_iota(jnp.int32, x.shape, 1)
        low = lane < n_chunk
        s_low = jnp.sum(jnp.where(low, sq, 0.0), axis=-1, keepdims=True)
        s_all = jnp.sum(sq, axis=-1, keepdims=True)
        ss = jnp.where(low, s_low, s_all - s_low)
    return x * lax.rsqrt(ss * (1.0 / n_chunk) + EPS) * gain


def _rope(x, cos_t, sin_t, n_chunk):
    half = n_chunk // 2
    if n_chunk == LANES:
        partner = pltpu.roll(x, half, 1)
    else:
        lane = lax.broadcasted_iota(jnp.int32, x.shape, 1)
        first = (lane % n_chunk) < half
        partner = jnp.where(first, pltpu.roll(x, LANES - half, 1), pltpu.roll(x, half, 1))
    return x * cos_t + partner * sin_t


def _split3(v):
    hi = v.astype(BF16)
    r1 = v - hi.astype(F32)
    mid = r1.astype(BF16)
    lo = (r1 - mid.astype(F32)).astype(BF16)
    return hi, mid, lo


def _even_prep_kernel(qa_ref, ka_ref, va_ref, qb_ref, kb_ref, vb_ref, fb_ref, cos_ref, sin_ref,
                      gdq_ref, gdk_ref, gfq_ref, gfk_ref, bf_ref,
                      qa_o, ka_o, va_o, qf_o, kf_o, vf_o, carry_ref, *, n_heads):
    i = pl.program_id(0)
    ts = qa_ref.shape[0]
    cos_t, sin_t = cos_ref[...], sin_ref[...]
    lane = lax.broadcasted_iota(jnp.int32, (ts, LANES), 1)
    ones_aug = jnp.where(lane < 3, 1.0, 0.0).astype(BF16)
    ones = jnp.ones((ts, LANES), BF16)

    z = fb_ref[...] + bf_ref[...]
    logf = jnp.minimum(z, 0.0) - jnp.log(1.0 + jnp.exp(-jnp.abs(z)))
    r = lax.broadcasted_iota(jnp.int32, (ts, ts), 0)
    c = lax.broadcasted_iota(jnp.int32, (ts, ts), 1)
    tri = jnp.where(c <= r, 1.0, 0.0).astype(BF16)

    @pl.when(i == 0)
    def _():
        carry_ref[...] = jnp.zeros_like(carry_ref)

    f_cum = carry_ref[0:1, :]
    for part in _split3(logf):
        f_cum = f_cum + jnp.dot(tri, part, preferred_element_type=F32)
    carry_ref[...] = jnp.broadcast_to(f_cum[ts - 1:ts, :], carry_ref.shape)
    neg_hi, neg_mid, neg_lo = _split3(f_cum * (-LOG2E))
    prow = lax.broadcasted_iota(jnp.int32, (LANES, LANES), 0)
    pcol = lax.broadcasted_iota(jnp.int32, (LANES, LANES), 1)

    for h in range(n_heads):
        sl = slice(h * LANES, (h + 1) * LANES)
        qa = _rope(_rms(qa_ref[:, sl], DIFF_QK_DIM, gdq_ref[...]), cos_t, sin_t, DIFF_QK_DIM)
        qa_o[h] = (qa * (DIFF_QK_DIM ** -0.5 * LOG2E)).astype(BF16)
        ka = _rope(_rms(ka_ref[:, sl], DIFF_QK_DIM, gdk_ref[...]), cos_t, sin_t, DIFF_QK_DIM)
        ka_o[h] = ka.astype(BF16)
        va_o[h, :, 0:LANES] = va_ref[:, sl].astype(BF16)
        va_o[h, :, LANES:2 * LANES] = ones
        qb = _rms(qb_ref[:, sl], HEAD_DIM, gfq_ref[...]) * (HEAD_DIM ** -0.5 * LOG2E)
        qf_o[h, :, 0:LANES] = qb.astype(BF16)
        qf_o[h, :, LANES:2 * LANES] = ones_aug
        kb = _rms(kb_ref[:, sl], HEAD_DIM, gfk_ref[...])
        kf_o[h, :, 0:LANES] = kb.astype(BF16)
        aug = jnp.zeros((ts, LANES), F32)
        for lane_idx, part in enumerate((neg_hi, neg_mid, neg_lo)):
            place = jnp.where((prow == h) & (pcol == lane_idx), 1.0, 0.0).astype(BF16)
            aug = aug + jnp.dot(part, place, preferred_element_type=F32)
        kf_o[h, :, LANES:2 * LANES] = aug.astype(BF16)
        vf_o[h, :, 0:LANES] = vb_ref[:, sl].astype(BF16)
        vf_o[h, :, LANES:2 * LANES] = ones


def _even_prep(proj, cos_t, sin_t, gain_diff, gain_fox, b_forget, n_heads):
    s = proj.shape[0]
    ts = 256
    w = n_heads * HEAD_DIM
    seg = lambda k: pl.BlockSpec((ts, w), lambda i, k=k: (i, k))
    tab = pl.BlockSpec((ts, LANES), lambda i: (i, 0))
    row = pl.BlockSpec((1, LANES), lambda i: (0, 0))
    gdq = jnp.tile(gain_diff[0], 2).reshape(1, LANES)
    gdk = jnp.tile(gain_diff[1], 2).reshape(1, LANES)
    bfp = jnp.zeros((1, LANES), F32).at[0, :n_heads].set(b_forget)
    narrow = jax.ShapeDtypeStruct((n_heads, s, LANES), BF16)
    wide = jax.ShapeDtypeStruct((n_heads, s, 2 * LANES), BF16)
    narrow_spec = pl.BlockSpec((n_heads, ts, LANES), lambda i: (0, i, 0))
    wide_spec = pl.BlockSpec((n_heads, ts, 2 * LANES), lambda i: (0, i, 0))
    return pl.pallas_call(
        functools.partial(_even_prep_kernel, n_heads=n_heads),
        grid=(s // ts,),
        in_specs=[seg(0), seg(1), seg(2), seg(3), seg(4), seg(5),
                  pl.BlockSpec((ts, LANES), lambda i: (i, 6 * w // LANES)),
                  tab, tab, row, row, row, row, row],
        out_specs=[narrow_spec, narrow_spec, wide_spec, wide_spec, wide_spec, wide_spec],
        out_shape=[narrow, narrow, wide, wide, wide, wide],
        scratch_shapes=[pltpu.VMEM((8, LANES), F32)],
        compiler_params=_params(("arbitrary",)),
    )(proj, proj, proj, proj, proj, proj, proj, cos_t, sin_t, gdq, gdk,
      gain_fox[0].reshape(1, LANES), gain_fox[1].reshape(1, LANES), bfp)


def _attn_kernel(*refs, mode, tq, tk, nrep, lam_init):
    if mode == "diff":
        q_ref, k_ref, v_ref, lam_ref, sub_ref, o_ref, qs_ref, m_ref, acc_ref = refs
    elif mode == "fox":
        q_ref, k_ref, v_ref, _, o_ref, m_ref, acc_ref = refs
    elif mode == "slc":
        q_ref, sel_ref, k_ref, v_ref, o_ref, qs_ref, m_ref, acc_ref = refs
    else:
        q_ref, k_ref, v_ref, o_ref, qs_ref, m_ref, acc_ref = refs
    i = pl.program_id(1)
    rows = nrep * tq
    q_lo = i * tq

    if mode == "diff":
        q = q_ref[0]
        lane = lax.broadcasted_iota(jnp.int32, q.shape, 1)
        qs_ref[0:tq, :] = jnp.where(lane < DIFF_QK_DIM, q, jnp.zeros_like(q))
        qs_ref[tq:2 * tq, :] = jnp.where(lane >= DIFF_QK_DIM, q, jnp.zeros_like(q))
    elif mode in ("slc", "win"):
        for hh in range(nrep):
            qs_ref[hh * tq:(hh + 1) * tq, 0:LANES] = q_ref[:, hh * LANES:(hh + 1) * LANES]
            if mode == "slc":
                qs_ref[hh * tq:(hh + 1) * tq, LANES:2 * LANES] = sel_ref[0]
    m_ref[...] = jnp.full(m_ref.shape, NEG, F32)
    acc_ref[...] = jnp.zeros(acc_ref.shape, F32)
    n_streams = m_ref.shape[0]

    def process(chunks):
        for c, masked, st in chunks:
            k0 = pl.multiple_of(c * tk, tk)
            k = k_ref[0, pl.ds(k0, tk), :]
            v = v_ref[0, pl.ds(k0, tk), :]
            q_all = q_ref[0] if mode == "fox" else qs_ref[...]
            s = lax.dot_general(q_all, k, NT_DIMS, preferred_element_type=F32)
            if masked:
                qpos = q_lo + lax.broadcasted_iota(jnp.int32, (rows, tk), 0) % tq
                kpos = k0 + lax.broadcasted_iota(jnp.int32, (rows, tk), 1)
                ok = kpos <= qpos
                if mode == "win":
                    ok = ok & (kpos > qpos - WINDOW)
                s = jnp.where(ok, s, NEG)
            tiles = [s[:, t * LANES:(t + 1) * LANES] for t in range(tk // LANES)]
            tile_max = tiles[0]
            for t in tiles[1:]:
                tile_max = jnp.maximum(tile_max, t)
            m_prev = m_ref[st]
            m_new = jnp.maximum(m_prev, jnp.max(tile_max, axis=-1, keepdims=True))
            alpha = jnp.exp2(m_prev - m_new)
            p = jnp.concatenate([jnp.exp2(t - m_new) for t in tiles], axis=1).astype(BF16)
            pv = jnp.dot(p, v, preferred_element_type=F32)
            acc_ref[st, :, 0:LANES] = alpha * acc_ref[st, :, 0:LANES] + pv[:, 0:LANES]
            acc_ref[st, :, LANES:2 * LANES] = alpha * acc_ref[st, :, LANES:2 * LANES] + pv[:, LANES:2 * LANES]
            m_ref[st] = m_new

    if mode == "win":
        @pl.when(i >= 2)
        def _():
            process([(i - 2, True, 0), (i - 1, False, 1), (i, True, 2)])

        @pl.when(i == 1)
        def _():
            process([(i - 1, False, 1), (i, True, 2)])

        @pl.when(i == 0)
        def _():
            process([(i, True, 2)])
    else:
        n_full = q_lo // tk

        def body(pair, carry):
            process([(2 * pair, False, 0), (2 * pair + 1, False, 1)])
            return carry

        lax.fori_loop(0, n_full // 2, body, 0)

        @pl.when(n_full % 2 == 1)
        def _():
            process([(n_full - 1, False, 0), (n_full, True, 1)])

        @pl.when(n_full % 2 == 0)
        def _():
            process([(n_full, True, 0)])

    m_all = m_ref[0]
    for st in range(1, n_streams):
        m_all = jnp.maximum(m_all, m_ref[st])
    num = jnp.zeros((rows, LANES), F32)
    den = jnp.zeros((rows, LANES), F32)
    for st in range(n_streams):
        w = jnp.exp2(m_ref[st] - m_all)
        num = num + w * acc_ref[st, :, 0:LANES]
        den = den + w * acc_ref[st, :, LANES:2 * LANES]
    o = num / den
    if mode == "diff":
        lp = lam_ref[...]
        lam = (jnp.exp(jnp.sum(lp[0:1] * lp[1:2], axis=-1, keepdims=True))
               - jnp.exp(jnp.sum(lp[2:3] * lp[3:4], axis=-1, keepdims=True)) + lam_init)
        od = o[0:tq] - lam * o[tq:2 * tq]
        ms = jnp.mean(od * od, axis=-1, keepdims=True)
        o_ref[...] = (od * lax.rsqrt(ms + EPS) * sub_ref[...] * (1.0 - lam_init)).astype(o_ref.dtype)
    elif mode == "fox":
        o_ref[...] = o.astype(o_ref.dtype)
    else:
        for hh in range(nrep):
            o_ref[:, hh * LANES:(hh + 1) * LANES] = o[hh * tq:(hh + 1) * tq].astype(o_ref.dtype)


def _attn_call(mode, n_outer, n_q, in_specs, out_spec, out_shape, scratch, args, *, tq, tk, nrep,
               lam_init=0.0, aliases=None):
    return pl.pallas_call(
        functools.partial(_attn_kernel, mode=mode, tq=tq, tk=tk, nrep=nrep, lam_init=lam_init),
        grid=(n_outer, n_q), in_specs=in_specs, out_specs=out_spec, out_shape=out_shape,
        scratch_shapes=scratch, input_output_aliases=aliases or {},
        compiler_params=_params(("arbitrary", "arbitrary")),
    )(*args)


def _softmax_scratch(rows, n_streams=2):
    return [pltpu.VMEM((n_streams, rows, LANES), F32), pltpu.VMEM((n_streams, rows, 2 * LANES), F32)]


def _diff_attention(qa, ka, va, lam_params, subln, lam_init, d_out):
    n_heads, s, _ = qa.shape
    tq, tk = 256, 512
    in_specs = [
        pl.BlockSpec((1, tq, LANES), lambda h, i: (h, i, 0)),
        pl.BlockSpec((1, s, LANES), lambda h, i: (h, 0, 0)),
        pl.BlockSpec((1, s, 2 * LANES), lambda h, i: (h, 0, 0)),
        pl.BlockSpec((4, DIFF_QK_DIM), lambda h, i: (0, 0)),
        pl.BlockSpec((1, LANES), lambda h, i: (0, 0)),
    ]
    out_spec = pl.BlockSpec((tq, LANES), lambda h, i: (i, h))
    scratch = [pltpu.VMEM((2 * tq, LANES), BF16)] + _softmax_scratch(2 * tq)
    return _attn_call("diff", n_heads, s // tq, in_specs, out_spec,
                      jax.ShapeDtypeStruct((s, d_out), BF16), scratch,
                      (qa, ka, va, lam_params, subln.reshape(1, LANES)),
                      tq=tq, tk=tk, nrep=2, lam_init=lam_init)


def _fox_attention(qf, kf, vf, o_prev, head_offset):
    n_heads, s, _ = qf.shape
    tq, tk = 512, 512
    in_specs = [
        pl.BlockSpec((1, tq, 2 * LANES), lambda h, i: (h, i, 0)),
        pl.BlockSpec((1, s, 2 * LANES), lambda h, i: (h, 0, 0)),
        pl.BlockSpec((1, s, 2 * LANES), lambda h, i: (h, 0, 0)),
        pl.BlockSpec(memory_space=pl.ANY),
    ]
    out_spec = pl.BlockSpec((tq, LANES), lambda h, i: (i, h + head_offset))
    return _attn_call("fox", n_heads, s // tq, in_specs, out_spec,
                      jax.ShapeDtypeStruct(o_prev.shape, o_prev.dtype), _softmax_scratch(tq),
                      (qf, kf, vf, o_prev), tq=tq, tk=tk, nrep=1, aliases={3: 0})


def _odd_prep_kernel(q_ref, kc_ref, vc_ref, ks_ref, vs_ref, kw_ref, vw_ref, gl_ref, cos_ref, sin_ref,
                     qg_ref, kg_ref, pe_ref,
                     qn_o, ktop_o, kbot_o, vtop_o, vbot_o, ksk_o, vs_o, kw_o, vw_o, gates_o,
                     *, n_heads, n_groups):
    i = pl.program_id(0)
    ts = q_ref.shape[0]
    cos_t, sin_t = cos_ref[...], sin_ref[...]
    ones = jnp.ones((ts, LANES), BF16)
    for h in range(n_heads):
        sl = slice(h * LANES, (h + 1) * LANES)
        q = _rope(_rms(q_ref[:, sl], HEAD_DIM, qg_ref[...]), cos_t, sin_t, HEAD_DIM)
        qn_o[:, sl] = (q * (HEAD_DIM ** -0.5 * LOG2E)).astype(BF16)
    key_block = (i * ts + lax.broadcasted_iota(jnp.int32, (ts, LANES), 0)) // SLC_BLOCK
    lane = lax.broadcasted_iota(jnp.int32, (ts, LANES), 1)
    block_onehot = jnp.where(lane == key_block, UNSELECTED_SCORE, 0.0).astype(BF16)
    for g in range(n_groups):
        sl = slice(g * LANES, (g + 1) * LANES)
        kc = _rope(kc_ref[:, sl], cos_t, sin_t, HEAD_DIM)
        ktop_o[:, sl] = (kc + pe_ref[0]).astype(BF16)
        kbot_o[:, sl] = (kc + pe_ref[1]).astype(BF16)
        vc = vc_ref[:, sl]
        vtop_o[:, sl] = (vc + pe_ref[2]).astype(BF16)
        vbot_o[:, sl] = (vc + pe_ref[3]).astype(BF16)
        ks = _rope(_rms(ks_ref[:, sl], HEAD_DIM, kg_ref[1:2, :]), cos_t, sin_t, HEAD_DIM)
        ksk_o[g, :, 0:LANES] = ks.astype(BF16)
        ksk_o[g, :, LANES:2 * LANES] = block_onehot
        kw = _rope(_rms(kw_ref[:, sl], HEAD_DIM, kg_ref[2:3, :]), cos_t, sin_t, HEAD_DIM)
        kw_o[g] = kw.astype(BF16)
        vs_o[g, :, 0:LANES] = vs_ref[:, sl].astype(BF16)
        vs_o[g, :, LANES:2 * LANES] = ones
        vw_o[g, :, 0:LANES] = vw_ref[:, sl].astype(BF16)
        vw_o[g, :, LANES:2 * LANES] = ones
    gates_o[...] = jax.nn.sigmoid(gl_ref[...])


def _odd_prep(proj, cos_t, sin_t, q_gain, k_gain, cmp_pos, n_heads, n_groups):
    s = proj.shape[0]
    ts = 256
    wq, wg = n_heads * HEAD_DIM, n_groups * HEAD_DIM
    seg = lambda k: pl.BlockSpec((ts, wg), lambda i, k=k: (i, wq // wg + k))
    tab = pl.BlockSpec((ts, LANES), lambda i: (i, 0))
    grp = pl.BlockSpec((ts, wg), lambda i: (i, 0))
    narrow_spec = pl.BlockSpec((n_groups, ts, LANES), lambda i: (0, i, 0))
    wide_spec = pl.BlockSpec((n_groups, ts, 2 * LANES), lambda i: (0, i, 0))
    pe_tiles = jnp.stack([jnp.tile(cmp_pos[kv, half * CMP_STRIDE:(half + 1) * CMP_STRIDE], (ts // CMP_STRIDE, 1))
                          for kv in range(2) for half in range(2)])
    g_bf = jax.ShapeDtypeStruct((s, wg), BF16)
    narrow = jax.ShapeDtypeStruct((n_groups, s, LANES), BF16)
    wide = jax.ShapeDtypeStruct((n_groups, s, 2 * LANES), BF16)
    return pl.pallas_call(
        functools.partial(_odd_prep_kernel, n_heads=n_heads, n_groups=n_groups),
        grid=(s // ts,),
        in_specs=[pl.BlockSpec((ts, wq), lambda i: (i, 0)), seg(0), seg(1), seg(2), seg(3), seg(4), seg(5),
                  pl.BlockSpec((ts, LANES), lambda i: (i, (wq + 6 * wg) // LANES)),
                  tab, tab,
                  pl.BlockSpec((1, LANES), lambda i: (0, 0)),
                  pl.BlockSpec((3, LANES), lambda i: (0, 0)),
                  pl.BlockSpec((4, ts, LANES), lambda i: (0, 0, 0))],
        out_specs=[pl.BlockSpec((ts, wq), lambda i: (i, 0)), grp, grp, grp, grp,
                   wide_spec, wide_spec, narrow_spec, wide_spec, tab],
        out_shape=[jax.ShapeDtypeStruct((s, wq), BF16), g_bf, g_bf, g_bf, g_bf,
                   wide, wide, narrow, wide, jax.ShapeDtypeStruct((s, LANES), F32)],
        compiler_params=_params(("arbitrary",)),
    )(proj, proj, proj, proj, proj, proj, proj, proj, cos_t, sin_t,
      q_gain.reshape(1, LANES), k_gain, pe_tiles)


def _compress_kernel(top_ref, bot_ref, w1_ref, w2_ref, kg_ref, o_ref):
    kv = pl.program_id(0)
    n_half = top_ref.shape[2]
    k_half = top_ref.shape[3]
    h_top = jnp.dot(top_ref[0, 0], w1_ref[0, 0:k_half, :].astype(BF16), preferred_element_type=F32)
    h_bot = jnp.dot(bot_ref[0, 0], w1_ref[0, k_half:2 * k_half, :].astype(BF16), preferred_element_type=F32)
    hid = h_top + pltpu.roll(h_bot, n_half - 1, 0)
    act = hid * jax.nn.sigmoid(hid)
    out = jnp.dot(act.astype(BF16), w2_ref[0].astype(BF16), preferred_element_type=F32)
    ms = jnp.mean(out * out, axis=-1, keepdims=True)
    normed = out * lax.rsqrt(ms + EPS) * kg_ref[...]
    o_ref[0, 0] = jnp.where(kv == 0, normed, out).astype(o_ref.dtype)


def _compress(top, bot, w1, w2, k_gain0):
    _, n_groups, n_half, k_half = top.shape
    hidden = w1.shape[2]
    blk = pl.BlockSpec((1, 1, n_half, k_half), lambda kv, g: (kv, g, 0, 0))
    return pl.pallas_call(
        _compress_kernel,
        grid=(2, n_groups),
        in_specs=[blk, blk,
                  pl.BlockSpec((1, 2 * k_half, hidden), lambda kv, g: (kv, 0, 0)),
                  pl.BlockSpec((1, hidden, HEAD_DIM), lambda kv, g: (kv, 0, 0)),
                  pl.BlockSpec((1, LANES), lambda kv, g: (0, 0))],
        out_specs=pl.BlockSpec((1, 1, n_half, HEAD_DIM), lambda kv, g: (kv, g, 0, 0)),
        out_shape=jax.ShapeDtypeStruct((2, n_groups, n_half, HEAD_DIM), BF16),
        compiler_params=_params(("arbitrary", "arbitrary")),
    )(top, bot, w1, w2, k_gain0.reshape(1, LANES))


def _cmp_select_kernel(q_ref, kc_ref, vc_ref, ov_ref, oc_ref, sel_ref, *, tq, heads_per_group, n_blocks_pad):
    i = pl.program_id(1)
    kc, vc = kc_ref[0, 0], vc_ref[0, 0]
    n_cmp = kc.shape[0]
    tpos = i * tq + lax.broadcasted_iota(jnp.int32, (tq, n_cmp), 0)
    cidx = lax.broadcasted_iota(jnp.int32, (tq, n_cmp), 1)
    valid = cidx * CMP_STRIDE + (CMP_BLOCK - 1) <= tpos
    p_sum = jnp.zeros((tq, n_cmp), F32)
    for hh in range(heads_per_group):
        sl = slice(hh * LANES, (hh + 1) * LANES)
        s = lax.dot_general(q_ref[:, sl], kc, NT_DIMS, preferred_element_type=F32)
        s = jnp.where(valid, s, NEG)
        e = jnp.where(valid, jnp.exp2(s - jnp.max(s, axis=-1, keepdims=True)), 0.0)
        p = e * (1.0 / jnp.maximum(jnp.sum(e, axis=-1, keepdims=True), TINY))
        oc_ref[:, sl] = jnp.dot(p.astype(BF16), vc, preferred_element_type=F32)
        p_sum = p_sum + p
    p_hi = p_sum.astype(BF16)
    p_lo = (p_sum - p_hi.astype(F32)).astype(BF16)
    imp = (lax.dot_general(ov_ref[...], p_hi, NT_DIMS, preferred_element_type=F32)
           + lax.dot_general(ov_ref[...], p_lo, NT_DIMS, preferred_element_type=F32))
    blk = lax.broadcasted_iota(jnp.int32, (n_blocks_pad, tq), 0).astype(F32)
    cur = ((i * tq + lax.broadcasted_iota(jnp.int32, (n_blocks_pad, tq), 1)) // SLC_BLOCK).astype(F32)
    forced = (blk == 0.0) | (blk == cur) | (blk == cur - 1.0)
    imp = jnp.where(forced, BIG, imp)
    imp = jnp.where(blk > cur, NEG, imp)
    sel = jnp.zeros((n_blocks_pad, tq), F32)
    for _ in range(SLC_TOPK):
        top = jnp.max(imp, axis=0, keepdims=True)
        first = jnp.min(jnp.where(imp == top, blk, float(n_blocks_pad)), axis=0, keepdims=True)
        hit = blk == first
        sel = jnp.where(hit, 1.0, sel)
        imp = jnp.where(hit, -jnp.inf, imp)
    sel_ref[0] = (sel.T - 1.0).astype(sel_ref.dtype)


def _cmp_select(qn, kcv, n_groups, heads_per_group):
    s = qn.shape[0]
    tq = 512
    n_cmp = kcv.shape[2]
    n_blocks_pad = LANES
    assert s // SLC_BLOCK <= n_blocks_pad
    cmp_start = np.arange(n_cmp) * CMP_STRIDE
    slc_start = np.arange(n_blocks_pad) * SLC_BLOCK
    overlap_t = ((cmp_start[None, :] <= slc_start[:, None] + SLC_BLOCK - 1)
                 & (cmp_start[None, :] + CMP_BLOCK - 1 >= slc_start[:, None]))
    overlap_t = jnp.asarray(overlap_t.astype(np.float32), dtype=BF16)
    wq = heads_per_group * HEAD_DIM
    return pl.pallas_call(
        functools.partial(_cmp_select_kernel, tq=tq, heads_per_group=heads_per_group, n_blocks_pad=n_blocks_pad),
        grid=(n_groups, s // tq),
        in_specs=[pl.BlockSpec((tq, wq), lambda g, i: (i, g)),
                  pl.BlockSpec((1, 1, n_cmp, HEAD_DIM), lambda g, i: (0, g, 0, 0)),
                  pl.BlockSpec((1, 1, n_cmp, HEAD_DIM), lambda g, i: (1, g, 0, 0)),
                  pl.BlockSpec((n_blocks_pad, n_cmp), lambda g, i: (0, 0))],
        out_specs=[pl.BlockSpec((tq, wq), lambda g, i: (i, g)),
                   pl.BlockSpec((1, tq, n_blocks_pad), lambda g, i: (g, i, 0))],
        out_shape=[jax.ShapeDtypeStruct(qn.shape, F32),
                   jax.ShapeDtypeStruct((n_groups, s, n_blocks_pad), BF16)],
        compiler_params=_params(("arbitrary", "arbitrary")),
    )(qn, kcv, kcv, overlap_t)


def _slc_attention(qn, sel, ksk, vs, heads_per_group):
    n_groups, s, _ = ksk.shape
    tq, tk = 128, 256
    wq = heads_per_group * HEAD_DIM
    in_specs = [
        pl.BlockSpec((tq, wq), lambda g, i: (i, g)),
        pl.BlockSpec((1, tq, LANES), lambda g, i: (g, i, 0)),
        pl.BlockSpec((1, s, 2 * LANES), lambda g, i: (g, 0, 0)),
        pl.BlockSpec((1, s, 2 * LANES), lambda g, i: (g, 0, 0)),
    ]
    out_spec = pl.BlockSpec((tq, wq), lambda g, i: (i, g))
    rows = heads_per_group * tq
    scratch = [pltpu.VMEM((rows, 2 * LANES), BF16)] + _softmax_scratch(rows)
    return _attn_call("slc", n_groups, s // tq, in_specs, out_spec, jax.ShapeDtypeStruct(qn.shape, F32),
                      scratch, (qn, sel, ksk, vs), tq=tq, tk=tk, nrep=heads_per_group)


def _win_attention(qn, kw, vw, heads_per_group):
    n_groups, s, _ = kw.shape
    tq = tk = 256
    assert 2 * tk == WINDOW
    wq = heads_per_group * HEAD_DIM
    in_specs = [
        pl.BlockSpec((tq, wq), lambda g, i: (i, g)),
        pl.BlockSpec((1, s, LANES), lambda g, i: (g, 0, 0)),
        pl.BlockSpec((1, s, 2 * LANES), lambda g, i: (g, 0, 0)),
    ]
    out_spec = pl.BlockSpec((tq, wq), lambda g, i: (i, g))
    rows = heads_per_group * tq
    scratch = [pltpu.VMEM((rows, LANES), BF16)] + _softmax_scratch(rows, n_streams=3)
    return _attn_call("win", n_groups, s // tq, in_specs, out_spec, jax.ShapeDtypeStruct(qn.shape, F32),
                      scratch, (qn, kw, vw), tq=tq, tk=tk, nrep=heads_per_group)


def _combine_kernel(oc_ref, os_ref, ow_ref, g_ref, o_ref, *, n_heads):
    g = g_ref[...]
    for h in range(n_heads):
        sl = slice(h * LANES, (h + 1) * LANES)
        o = (g[:, 3 * h:3 * h + 1] * oc_ref[:, sl] + g[:, 3 * h + 1:3 * h + 2] * os_ref[:, sl]
             + g[:, 3 * h + 2:3 * h + 3] * ow_ref[:, sl])
        o_ref[:, sl] = o.astype(o_ref.dtype)


def _combine(oc, osel, ow, gates, n_heads):
    s, w = oc.shape
    ts = 256
    big = pl.BlockSpec((ts, w), lambda i: (i, 0))
    return pl.pallas_call(
        functools.partial(_combine_kernel, n_heads=n_heads),
        grid=(s // ts,),
        in_specs=[big, big, big, pl.BlockSpec((ts, LANES), lambda i: (i, 0))],
        out_specs=big,
        out_shape=jax.ShapeDtypeStruct((s, w), BF16),
        compiler_params=_params(("arbitrary",)),
    )(oc, osel, ow, gates)


def _rope_tables(positions, dim):
    inv = ROPE_THETA ** (-jnp.arange(0, dim, 2, dtype=F32) / dim)
    ang = positions.astype(F32)[:, None] * inv
    cos, sin = jnp.cos(ang), jnp.sin(ang)
    reps = LANES // dim
    cos_t = jnp.tile(jnp.concatenate([cos, cos], axis=-1), (1, reps))
    sin_t = jnp.tile(jnp.concatenate([-sin, sin], axis=-1), (1, reps))
    return cos_t, sin_t


def _pad_cols(w, n):
    return jnp.pad(w, ((0, 0), (0, n - w.shape[1])))


def _even_mixer(h, w_in, b_forget, gain_diff, gain_fox, lam_params, subln, tables, lam_init):
    d_model = h.shape[1]
    n_heads = d_model // (2 * HEAD_DIM)
    n_in = 6 * n_heads * HEAD_DIM + 1024
    proj = _matmul(h, _pad_cols(w_in, n_in).astype(BF16), bm=1024, bn=1024, bk=d_model)
    qa, ka, va, qf, kf, vf = _even_prep(proj, *tables, gain_diff, gain_fox, b_forget, n_heads)
    o = _diff_attention(qa, ka, va, lam_params, subln, lam_init, d_model)
    return _fox_attention(qf, kf, vf, o, n_heads)


def _odd_mixer(h, w_in, q_gain, k_gain, cmp_pos, cmp_w1, cmp_w2, tables):
    s, d_model = h.shape
    n_heads, n_groups = d_model // HEAD_DIM, N_KV_GROUPS
    n_in = d_model + 7 * n_groups * HEAD_DIM
    proj = _matmul(h, _pad_cols(w_in, n_in).astype(BF16), bm=1024, bn=n_in // 10, bk=d_model)
    (qn, ktop, kbot, vtop, vbot, ksk, vs, kw, vw, gates) = _odd_prep(
        proj, *tables, q_gain, k_gain, cmp_pos, n_heads, n_groups)

    def halves(k_part, v_part):
        a = jnp.stack([k_part, v_part]).reshape(2, s // CMP_STRIDE, CMP_STRIDE, n_groups, HEAD_DIM)
        return jnp.transpose(a, (0, 3, 1, 2, 4)).reshape(2, n_groups, s // CMP_STRIDE, CMP_STRIDE * HEAD_DIM)

    kcv = _compress(halves(ktop, vtop), halves(kbot, vbot), cmp_w1, cmp_w2, k_gain[0])
    hpg = n_heads // n_groups
    oc, sel = _cmp_select(qn, kcv, n_groups, hpg)
    osel = _slc_attention(qn, sel, ksk, vs, hpg)
    ow = _win_attention(qn, kw, vw, hpg)
    return _combine(oc, osel, ow, gates, n_heads)


def kernel(x, c, positions, w_ada, b_ada, norm_gain, even_w_in, even_b_forget, even_qk_gain_diff,
           even_qk_gain_fox, even_diff_lambda, even_diff_subln, even_w_out, odd_w_in, odd_q_gain,
           odd_k_gain, odd_cmp_pos, odd_cmp_w1, odd_cmp_w2, odd_w_out, mlp_w1, mlp_w2):
    batch, s, d_model = x.shape
    assert batch == 1
    depth = w_ada.shape[0]
    xs = x[0]
    pos = positions[0]
    tables64 = _rope_tables(pos, DIFF_QK_DIM)
    tables128 = _rope_tables(pos, HEAD_DIM)
    mod = _adaln(c, w_ada, b_ada)
    for i in range(depth):
        sh1, sc1, g1, sh2, sc2, g2 = [mod[i, :, k * d_model:(k + 1) * d_model] for k in range(6)]
        h = _norm_mod(xs, norm_gain[i, 0], sc1, sh1)
        if i % 2 == 0:
            e = i // 2
            lam_init = 0.8 - 0.6 * math.exp(-0.3 * i)
            o = _even_mixer(h, even_w_in[e], even_b_forget[e], even_qk_gain_diff[e], even_qk_gain_fox[e],
                            even_diff_lambda[e], even_diff_subln[e], tables64, lam_init)
            w_out = even_w_out[e]
        else:
            od = i // 2
            o = _odd_mixer(h, odd_w_in[od], odd_q_gain[od], odd_k_gain[od], odd_cmp_pos[od],
                           odd_cmp_w1[od], odd_cmp_w2[od], tables128)
            w_out = odd_w_out[od]
        xs = _matmul(o, w_out.astype(BF16), bm=512, bn=1024, bk=d_model, epilogue="resid", resid=xs, gate=g1)
        h = _norm_mod(xs, norm_gain[i, 1], sc2, sh2)
        ff = _matmul(h, mlp_w1[i].astype(BF16), bm=1024, bn=1024, bk=d_model, epilogue="relu2", out_dtype=BF16)
        xs = _matmul(ff, mlp_w2[i].astype(BF16), bm=1024, bn=1024, bk=2048, epilogue="resid", resid=xs, gate=g2)
    return xs[None]
```

```python
import functools
import math

import numpy as np
import jax
import jax.numpy as jnp
from jax import lax
from jax.experimental import pallas as pl
from jax.experimental.pallas import tpu as pltpu

F32 = jnp.float32
BF16 = jnp.bfloat16

HEAD_DIM = 128
DIFF_QK_DIM = 64
N_KV_GROUPS = 4
CMP_BLOCK = 32
CMP_STRIDE = 16
SLC_BLOCK = 64
SLC_TOPK = 16
WINDOW = 512
ROPE_THETA = 10000.0
EPS = 1e-6
NEG = -1e30
BIG = 1e30
TINY = 1e-20
UNSELECTED_SCORE = 2.0 ** 99
LOG2E = 1.4426950408889634
LANES = 128
VMEM_LIMIT = 56 * 1024 * 1024

NT_DIMS = (((1,), (1,)), ((), ()))


def _params(sem):
    return pltpu.CompilerParams(dimension_semantics=sem, vmem_limit_bytes=VMEM_LIMIT)


def _adaln_kernel(cb_ref, w_ref, b_ref, o_ref):
    d_model, tn = w_ref.shape[1], w_ref.shape[2]
    rows_per_step = 256

    def body(r, acc):
        rows = pl.ds(pl.multiple_of(r * rows_per_step, rows_per_step), rows_per_step)
        cb = cb_ref[rows, :]
        cond = cb * jax.nn.sigmoid(cb)
        prod = w_ref[0, rows, :] * jnp.concatenate([cond] * (tn // LANES), axis=1)
        return acc + prod.reshape(rows_per_step // 8, 8, tn).sum(axis=0)

    acc = lax.fori_loop(0, d_model // rows_per_step, body, jnp.zeros((8, tn), F32))
    o_ref[0] = acc.sum(axis=0, keepdims=True) + b_ref[0]


def _adaln(c, w_ada, b_ada):
    depth, d_model, n_out = w_ada.shape
    tn = 512
    cb = jnp.broadcast_to(c.reshape(d_model, 1), (d_model, LANES))
    return pl.pallas_call(
        _adaln_kernel,
        grid=(depth, n_out // tn),
        in_specs=[
            pl.BlockSpec((d_model, LANES), lambda l, n: (0, 0)),
            pl.BlockSpec((1, d_model, tn), lambda l, n: (l, 0, n)),
            pl.BlockSpec((1, 1, tn), lambda l, n: (l, 0, n)),
        ],
        out_specs=pl.BlockSpec((1, 1, tn), lambda l, n: (l, 0, n)),
        out_shape=jax.ShapeDtypeStruct((depth, 1, n_out), F32),
        compiler_params=_params(("arbitrary", "arbitrary")),
    )(cb, w_ada, b_ada.reshape(depth, 1, n_out))


def _norm_mod_kernel(x_ref, g_ref, sc_ref, sh_ref, o_ref):
    x = x_ref[...]
    ms = jnp.mean(x * x, axis=-1, keepdims=True)
    y = x * lax.rsqrt(ms + EPS) * g_ref[...]
    o_ref[...] = (y * (1.0 + sc_ref[...]) + sh_ref[...]).astype(o_ref.dtype)


def _norm_mod(x, gain, scale, shift):
    s, d_model = x.shape
    ts = 256
    row = pl.BlockSpec((1, d_model), lambda i: (0, 0))
    return pl.pallas_call(
        _norm_mod_kernel,
        grid=(s // ts,),
        in_specs=[pl.BlockSpec((ts, d_model), lambda i: (i, 0)), row, row, row],
        out_specs=pl.BlockSpec((ts, d_model), lambda i: (i, 0)),
        out_shape=jax.ShapeDtypeStruct((s, d_model), BF16),
        compiler_params=_params(("arbitrary",)),
    )(x, gain.reshape(1, d_model), scale, shift)


def _mm_kernel(*refs, nk, epilogue):
    if epilogue == "resid":
        a_ref, b_ref, x_ref, g_ref, o_ref = refs[:5]
    else:
        a_ref, b_ref, o_ref = refs[:3]

    def finish(acc):
        if epilogue == "plain":
            o_ref[...] = acc.astype(o_ref.dtype)
        elif epilogue == "relu2":
            r = jnp.maximum(acc, 0.0)
            o_ref[...] = (r * r).astype(o_ref.dtype)
        else:
            o_ref[...] = x_ref[...] + g_ref[...] * acc

    part = jnp.dot(a_ref[...], b_ref[...], preferred_element_type=F32)
    if nk == 1:
        finish(part)
    else:
        acc_ref = refs[-1]
        k = pl.program_id(2)

        @pl.when(k == 0)
        def _():
            acc_ref[...] = part

        @pl.when(k > 0)
        def _():
            acc_ref[...] += part

        @pl.when(k == nk - 1)
        def _():
            finish(acc_ref[...])


def _matmul(a, b, *, bm, bn, bk, epilogue="plain", out_dtype=F32, resid=None, gate=None):
    m, kdim = a.shape
    n = b.shape[1]
    nk = kdim // bk
    in_specs = [pl.BlockSpec((bm, bk), lambda i, j, k: (i, k)),
                pl.BlockSpec((bk, bn), lambda i, j, k: (k, j))]
    args = [a, b]
    if epilogue == "resid":
        in_specs += [pl.BlockSpec((bm, bn), lambda i, j, k: (i, j)),
                     pl.BlockSpec((1, bn), lambda i, j, k: (0, j))]
        args += [resid, gate]
    scratch = [pltpu.VMEM((bm, bn), F32)] if nk > 1 else []
    return pl.pallas_call(
        functools.partial(_mm_kernel, nk=nk, epilogue=epilogue),
        grid=(m // bm, n // bn, nk),
        in_specs=in_specs,
        out_specs=pl.BlockSpec((bm, bn), lambda i, j, k: (i, j)),
        out_shape=jax.ShapeDtypeStruct((m, n), out_dtype),
        scratch_shapes=scratch,
        compiler_params=_params(("arbitrary", "arbitrary", "arbitrary")),
    )(*args)


def _rms(x, n_chunk, gain):
    sq = x * x
    if n_chunk == LANES:
        ss = jnp.sum(sq, axis=-1, keepdims=True)
    else:
        lane = lax.broadcasted_iota(jnp.int32, x.shape, 1)
        low = lane < n_chunk
        s_low = jnp.sum(jnp.where(low, sq, 0.0), axis=-1, keepdims=True)
        s_all = jnp.sum(sq, axis=-1, keepdims=True)
        ss = jnp.where(low, s_low, s_all - s_low)
    return x * lax.rsqrt(ss * (1.0 / n_chunk) + EPS) * gain


def _rope(x, cos_t, sin_t, n_chunk):
    half = n_chunk // 2
    if n_chunk == LANES:
        partner = pltpu.roll(x, half, 1)
    else:
        lane = lax.broadcasted_iota(jnp.int32, x.shape, 1)
        first = (lane % n_chunk) < half
        partner = jnp.where(first, pltpu.roll(x, LANES - half, 1), pltpu.roll(x, half, 1))
    return x * cos_t + partner * sin_t


def _split3(v):
    hi = v.astype(BF16)
    r1 = v - hi.astype(F32)
    mid = r1.astype(BF16)
    lo = (r1 - mid.astype(F32)).astype(BF16)
    return hi, mid, lo


def _even_prep_kernel(qa_ref, ka_ref, va_ref, qb_ref, kb_ref, vb_ref, fb_ref, cos_ref, sin_ref,
                      gdq_ref, gdk_ref, gfq_ref, gfk_ref, bf_ref,
                      qa_o, ka_o, va_o, qf_o, kf_o, vf_o, carry_ref, *, n_heads):
    i = pl.program_id(0)
    ts = qa_ref.shape[0]
    cos_t, sin_t = cos_ref[...], sin_ref[...]
    lane = lax.broadcasted_iota(jnp.int32, (ts, LANES), 1)
    ones_aug = jnp.where(lane < 3, 1.0, 0.0).astype(BF16)
    ones = jnp.ones((ts, LANES), BF16)

    z = fb_ref[...] + bf_ref[...]
    logf = jnp.minimum(z, 0.0) - jnp.log(1.0 + jnp.exp(-jnp.abs(z)))
    r = lax.broadcasted_iota(jnp.int32, (ts, ts), 0)
    c = lax.broadcasted_iota(jnp.int32, (ts, ts), 1)
    tri = jnp.where(c <= r, 1.0, 0.0).astype(BF16)

    @pl.when(i == 0)
    def _():
        carry_ref[...] = jnp.zeros_like(carry_ref)

    f_cum = carry_ref[0:1, :]
    for part in _split3(logf):
        f_cum = f_cum + jnp.dot(tri, part, preferred_element_type=F32)
    carry_ref[...] = jnp.broadcast_to(f_cum[ts - 1:ts, :], carry_ref.shape)
    neg_hi, neg_mid, neg_lo = _split3(f_cum * (-LOG2E))
    prow = lax.broadcasted_iota(jnp.int32, (LANES, LANES), 0)
    pcol = lax.broadcasted_iota(jnp.int32, (LANES, LANES), 1)

    for h in range(n_heads):
        sl = slice(h * LANES, (h + 1) * LANES)
        qa = _rope(_rms(qa_ref[:, sl], DIFF_QK_DIM, gdq_ref[...]), cos_t, sin_t, DIFF_QK_DIM)
        qa_o[h] = (qa * (DIFF_QK_DIM ** -0.5 * LOG2E)).astype(BF16)
        ka = _rope(_rms(ka_ref[:, sl], DIFF_QK_DIM, gdk_ref[...]), cos_t, sin_t, DIFF_QK_DIM)
        ka_o[h] = ka.astype(BF16)
        va_o[h, :, 0:LANES] = va_ref[:, sl].astype(BF16)
        va_o[h, :, LANES:2 * LANES] = ones
        qb = _rms(qb_ref[:, sl], HEAD_DIM, gfq_ref[...]) * (HEAD_DIM ** -0.5 * LOG2E)
        qf_o[h, :, 0:LANES] = qb.astype(BF16)
        qf_o[h, :, LANES:2 * LANES] = ones_aug
        kb = _rms(kb_ref[:, sl], HEAD_DIM, gfk_ref[...])
        kf_o[h, :, 0:LANES] = kb.astype(BF16)
        aug = jnp.zeros((ts, LANES), F32)
        for lane_idx, part in enumerate((neg_hi, neg_mid, neg_lo)):
            place = jnp.where((prow == h) & (pcol == lane_idx), 1.0, 0.0).astype(BF16)
            aug = aug + jnp.dot(part, place, preferred_element_type=F32)
        kf_o[h, :, LANES:2 * LANES] = aug.astype(BF16)
        vf_o[h, :, 0:LANES] = vb_ref[:, sl].astype(BF16)
        vf_o[h, :, LANES:2 * LANES] = ones


def _even_prep(proj, cos_t, sin_t, gain_diff, gain_fox, b_forget, n_heads):
    s = proj.shape[0]
    ts = 256
    w = n_heads * HEAD_DIM
    seg = lambda k: pl.BlockSpec((ts, w), lambda i, k=k: (i, k))
    tab = pl.BlockSpec((ts, LANES), lambda i: (i, 0))
    row = pl.BlockSpec((1, LANES), lambda i: (0, 0))
    gdq = jnp.tile(gain_diff[0], 2).reshape(1, LANES)
    gdk = jnp.tile(gain_diff[1], 2).reshape(1, LANES)
    bfp = jnp.zeros((1, LANES), F32).at[0, :n_heads].set(b_forget)
    narrow = jax.ShapeDtypeStruct((n_heads, s, LANES), BF16)
    wide = jax.ShapeDtypeStruct((n_heads, s, 2 * LANES), BF16)
    narrow_spec = pl.BlockSpec((n_heads, ts, LANES), lambda i: (0, i, 0))
    wide_spec = pl.BlockSpec((n_heads, ts, 2 * LANES), lambda i: (0, i, 0))
    return pl.pallas_call(
        functools.partial(_even_prep_kernel, n_heads=n_heads),
        grid=(s // ts,),
        in_specs=[seg(0), seg(1), seg(2), seg(3), seg(4), seg(5),
                  pl.BlockSpec((ts, LANES), lambda i: (i, 6 * w // LANES)),
                  tab, tab, row, row, row, row, row],
        out_specs=[narrow_spec, narrow_spec, wide_spec, wide_spec, wide_spec, wide_spec],
        out_shape=[narrow, narrow, wide, wide, wide, wide],
        scratch_shapes=[pltpu.VMEM((8, LANES), F32)],
        compiler_params=_params(("arbitrary",)),
    )(proj, proj, proj, proj, proj, proj, proj, cos_t, sin_t, gdq, gdk,
      gain_fox[0].reshape(1, LANES), gain_fox[1].reshape(1, LANES), bfp)


def _attn_kernel(*refs, mode, tq, tk, nrep, n_group, lam_init):
    if mode == "diff":
        q_ref, k_ref, v_ref, lam_ref, sub_ref, o_ref, qs_ref, m_ref, acc_ref = refs
    elif mode == "fox":
        q_ref, k_ref, v_ref, _, o_ref, m_ref, acc_ref = refs
    elif mode == "slc":
        q_ref, sel_ref, k_ref, v_ref, prev_ref, gate_ref, o_ref, qs_ref, m_ref, acc_ref = refs
    else:
        q_ref, k_ref, v_ref, prev_ref, gate_ref, o_ref, qs_ref, m_ref, acc_ref = refs
    i = pl.program_id(1)
    rows = nrep * tq
    q_lo = i * tq
    blk = min(tq, tk)
    n_blk = tq // blk

    def row0(b, rep):
        return (b * nrep + rep) * blk

    for b in range(n_blk):
        qrows = slice(b * blk, (b + 1) * blk)
        if mode == "diff":
            q = q_ref[0, qrows, :]
            lane = lax.broadcasted_iota(jnp.int32, q.shape, 1)
            qs_ref[row0(b, 0):row0(b, 0) + blk, :] = jnp.where(lane < DIFF_QK_DIM, q, jnp.zeros_like(q))
            qs_ref[row0(b, 1):row0(b, 1) + blk, :] = jnp.where(lane >= DIFF_QK_DIM, q, jnp.zeros_like(q))
        elif mode in ("slc", "win"):
            for hh in range(nrep):
                dst = slice(row0(b, hh), row0(b, hh) + blk)
                qs_ref[dst, 0:LANES] = q_ref[qrows, hh * LANES:(hh + 1) * LANES]
                if mode == "slc":
                    qs_ref[dst, LANES:2 * LANES] = sel_ref[0, qrows, :]
    m_ref[...] = jnp.full(m_ref.shape, NEG, F32)
    acc_ref[...] = jnp.zeros(acc_ref.shape, F32)
    n_streams = m_ref.shape[0]

    def process(chunks):
        for c, masked, st, r0 in chunks:
            nrows = rows - r0
            k0 = pl.multiple_of(c * tk, tk)
            k = k_ref[0, pl.ds(k0, tk), :]
            v = v_ref[0, pl.ds(k0, tk), :]
            q_all = q_ref[0, r0:rows, :] if mode == "fox" else qs_ref[r0:rows, :]
            s = lax.dot_general(q_all, k, NT_DIMS, preferred_element_type=F32)
            if masked:
                r = r0 + lax.broadcasted_iota(jnp.int32, (nrows, tk), 0)
                qpos = q_lo + (r // (nrep * blk)) * blk + r % blk
                kpos = k0 + lax.broadcasted_iota(jnp.int32, (nrows, tk), 1)
                ok = kpos <= qpos
                if mode == "win":
                    ok = ok & (kpos > qpos - WINDOW)
                s = jnp.where(ok, s, NEG)
            tiles = [s[:, t * LANES:(t + 1) * LANES] for t in range(tk // LANES)]
            tile_max = tiles[0]
            for t in tiles[1:]:
                tile_max = jnp.maximum(tile_max, t)
            m_prev = m_ref[st, r0:rows, :]
            m_new = jnp.maximum(m_prev, jnp.max(tile_max, axis=-1, keepdims=True))
            alpha = jnp.exp2(m_prev - m_new)
            p = jnp.concatenate([jnp.exp2(t - m_new) for t in tiles], axis=1).astype(BF16)
            pv = jnp.dot(p, v, preferred_element_type=F32)
            acc_ref[st, r0:rows, 0:LANES] = alpha * acc_ref[st, r0:rows, 0:LANES] + pv[:, 0:LANES]
            acc_ref[st, r0:rows, LANES:2 * LANES] = (alpha * acc_ref[st, r0:rows, LANES:2 * LANES]
                                                     + pv[:, LANES:2 * LANES])
            m_ref[st, r0:rows, :] = m_new

    if mode == "win":
        @pl.when(i >= 2)
        def _():
            process([(i - 2, True, 0, 0), (i - 1, False, 1, 0), (i, True, 2, 0)])

        @pl.when(i == 1)
        def _():
            process([(i - 1, False, 1, 0), (i, True, 2, 0)])

        @pl.when(i == 0)
        def _():
            process([(i, True, 2, 0)])
    else:
        n_full = q_lo // tk
        n_grouped = (n_full // n_group) * n_group

        def group_body(g, carry):
            process([(g * n_group + j, False, j, 0) for j in range(n_group)])
            return carry

        def single_body(c, carry):
            process([(c, False, 0, 0)])
            return carry

        lax.fori_loop(0, n_full // n_group, group_body, 0)
        lax.fori_loop(n_grouped, n_full, single_body, 0)
        process([(n_full + j, True, j, row0(j, 0)) for j in range(n_blk)])

    m_all = m_ref[0]
    for st in range(1, n_streams):
        m_all = jnp.maximum(m_all, m_ref[st])
    num = jnp.zeros((rows, LANES), F32)
    den = jnp.zeros((rows, LANES), F32)
    for st in range(n_streams):
        w = jnp.exp2(m_ref[st] - m_all)
        num = num + w * acc_ref[st, :, 0:LANES]
        den = den + w * acc_ref[st, :, LANES:2 * LANES]
    o = num / den
    if mode == "diff":
        lp = lam_ref[...]
        lam = (jnp.exp(jnp.sum(lp[0:1] * lp[1:2], axis=-1, keepdims=True))
               - jnp.exp(jnp.sum(lp[2:3] * lp[3:4], axis=-1, keepdims=True)) + lam_init)
    for b in range(n_blk):
        qrows = slice(b * blk, (b + 1) * blk)
        if mode == "diff":
            od = o[row0(b, 0):row0(b, 0) + blk] - lam * o[row0(b, 1):row0(b, 1) + blk]
            ms = jnp.mean(od * od, axis=-1, keepdims=True)
            o_ref[qrows, :] = (od * lax.rsqrt(ms + EPS) * sub_ref[...] * (1.0 - lam_init)).astype(o_ref.dtype)
        elif mode == "fox":
            o_ref[qrows, :] = o[qrows].astype(o_ref.dtype)
        else:
            branch = 1 if mode == "slc" else 2
            gates = gate_ref[0, qrows, :]
            for hh in range(nrep):
                cols = slice(hh * LANES, (hh + 1) * LANES)
                gate = gates[:, 3 * hh + branch:3 * hh + branch + 1]
                o_ref[qrows, cols] = (prev_ref[qrows, cols]
                                      + gate * o[row0(b, hh):row0(b, hh) + blk]).astype(o_ref.dtype)


def _attn_call(mode, n_outer, n_q, in_specs, out_spec, out_shape, scratch, args, *, tq, tk, nrep,
               n_group=1, lam_init=0.0, aliases=None):
    return pl.pallas_call(
        functools.partial(_attn_kernel, mode=mode, tq=tq, tk=tk, nrep=nrep, n_group=n_group, lam_init=lam_init),
        grid=(n_outer, n_q), in_specs=in_specs, out_specs=out_spec, out_shape=out_shape,
        scratch_shapes=scratch, input_output_aliases=aliases or {},
        compiler_params=_params(("arbitrary", "arbitrary")),
    )(*args)


def _softmax_scratch(rows, n_streams=2):
    return [pltpu.VMEM((n_streams, rows, LANES), F32), pltpu.VMEM((n_streams, rows, 2 * LANES), F32)]


def _diff_attention(qa, ka, va, lam_params, subln, lam_init, d_out):
    n_heads, s, _ = qa.shape
    tq, tk, n_group = 1024, 256, 4
    in_specs = [
        pl.BlockSpec((1, tq, LANES), lambda h, i: (h, i, 0)),
        pl.BlockSpec((1, s, LANES), lambda h, i: (h, 0, 0)),
        pl.BlockSpec((1, s, 2 * LANES), lambda h, i: (h, 0, 0)),
        pl.BlockSpec((4, DIFF_QK_DIM), lambda h, i: (0, 0)),
        pl.BlockSpec((1, LANES), lambda h, i: (0, 0)),
    ]
    out_spec = pl.BlockSpec((tq, LANES), lambda h, i: (i, h))
    scratch = [pltpu.VMEM((2 * tq, LANES), BF16)] + _softmax_scratch(2 * tq, max(n_group, tq // tk))
    return _attn_call("diff", n_heads, s // tq, in_specs, out_spec,
                      jax.ShapeDtypeStruct((s, d_out), BF16), scratch,
                      (qa, ka, va, lam_params, subln.reshape(1, LANES)),
                      tq=tq, tk=tk, nrep=2, n_group=n_group, lam_init=lam_init)


def _fox_attention(qf, kf, vf, o_prev, head_offset):
    n_heads, s, _ = qf.shape
    tq, tk, n_group = 1024, 256, 4
    in_specs = [
        pl.BlockSpec((1, tq, 2 * LANES), lambda h, i: (h, i, 0)),
        pl.BlockSpec((1, s, 2 * LANES), lambda h, i: (h, 0, 0)),
        pl.BlockSpec((1, s, 2 * LANES), lambda h, i: (h, 0, 0)),
        pl.BlockSpec(memory_space=pl.ANY),
    ]
    out_spec = pl.BlockSpec((tq, LANES), lambda h, i: (i, h + head_offset))
    return _attn_call("fox", n_heads, s // tq, in_specs, out_spec,
                      jax.ShapeDtypeStruct(o_prev.shape, o_prev.dtype),
                      _softmax_scratch(tq, max(n_group, tq // tk)),
                      (qf, kf, vf, o_prev), tq=tq, tk=tk, nrep=1, n_group=n_group, aliases={3: 0})


def _odd_prep_kernel(q_ref, kc_ref, vc_ref, ks_ref, vs_ref, kw_ref, vw_ref, gl_ref, cos_ref, sin_ref,
                     qg_ref, kg_ref, pe_ref,
                     qn_o, ktop_o, kbot_o, vtop_o, vbot_o, ksk_o, vs_o, kw_o, vw_o, gates_o,
                     *, n_heads, n_groups):
    i = pl.program_id(0)
    ts = q_ref.shape[0]
    cos_t, sin_t = cos_ref[...], sin_ref[...]
    ones = jnp.ones((ts, LANES), BF16)
    for h in range(n_heads):
        sl = slice(h * LANES, (h + 1) * LANES)
        q = _rope(_rms(q_ref[:, sl], HEAD_DIM, qg_ref[...]), cos_t, sin_t, HEAD_DIM)
        qn_o[:, sl] = (q * (HEAD_DIM ** -0.5 * LOG2E)).astype(BF16)
    key_block = (i * ts + lax.broadcasted_iota(jnp.int32, (ts, LANES), 0)) // SLC_BLOCK
    lane = lax.broadcasted_iota(jnp.int32, (ts, LANES), 1)
    block_onehot = jnp.where(lane == key_block, UNSELECTED_SCORE, 0.0).astype(BF16)
    for g in range(n_groups):
        sl = slice(g * LANES, (g + 1) * LANES)
        kc = _rope(kc_ref[:, sl], cos_t, sin_t, HEAD_DIM)
        ktop_o[:, sl] = (kc + pe_ref[0]).astype(BF16)
        kbot_o[:, sl] = (kc + pe_ref[1]).astype(BF16)
        vc = vc_ref[:, sl]
        vtop_o[:, sl] = (vc + pe_ref[2]).astype(BF16)
        vbot_o[:, sl] = (vc + pe_ref[3]).astype(BF16)
        ks = _rope(_rms(ks_ref[:, sl], HEAD_DIM, kg_ref[1:2, :]), cos_t, sin_t, HEAD_DIM)
        ksk_o[g, :, 0:LANES] = ks.astype(BF16)
        ksk_o[g, :, LANES:2 * LANES] = block_onehot
        kw = _rope(_rms(kw_ref[:, sl], HEAD_DIM, kg_ref[2:3, :]), cos_t, sin_t, HEAD_DIM)
        kw_o[g] = kw.astype(BF16)
        vs_o[g, :, 0:LANES] = vs_ref[:, sl].astype(BF16)
        vs_o[g, :, LANES:2 * LANES] = ones
        vw_o[g, :, 0:LANES] = vw_ref[:, sl].astype(BF16)
        vw_o[g, :, LANES:2 * LANES] = ones
    gates = jax.nn.sigmoid(gl_ref[...])
    lanes_per_group = 3 * n_heads // n_groups
    for g in range(n_groups):
        gates_o[g] = gates if g == 0 else pltpu.roll(gates, LANES - g * lanes_per_group, 1)


def _odd_prep(proj, cos_t, sin_t, q_gain, k_gain, cmp_pos, n_heads, n_groups):
    s = proj.shape[0]
    ts = 256
    wq, wg = n_heads * HEAD_DIM, n_groups * HEAD_DIM
    seg = lambda k: pl.BlockSpec((ts, wg), lambda i, k=k: (i, wq // wg + k))
    tab = pl.BlockSpec((ts, LANES), lambda i: (i, 0))
    grp = pl.BlockSpec((ts, wg), lambda i: (i, 0))
    narrow_spec = pl.BlockSpec((n_groups, ts, LANES), lambda i: (0, i, 0))
    wide_spec = pl.BlockSpec((n_groups, ts, 2 * LANES), lambda i: (0, i, 0))
    pe_tiles = jnp.stack([jnp.tile(cmp_pos[kv, half * CMP_STRIDE:(half + 1) * CMP_STRIDE], (ts // CMP_STRIDE, 1))
                          for kv in range(2) for half in range(2)])
    g_bf = jax.ShapeDtypeStruct((s, wg), BF16)
    narrow = jax.ShapeDtypeStruct((n_groups, s, LANES), BF16)
    wide = jax.ShapeDtypeStruct((n_groups, s, 2 * LANES), BF16)
    return pl.pallas_call(
        functools.partial(_odd_prep_kernel, n_heads=n_heads, n_groups=n_groups),
        grid=(s // ts,),
        in_specs=[pl.BlockSpec((ts, wq), lambda i: (i, 0)), seg(0), seg(1), seg(2), seg(3), seg(4), seg(5),
                  pl.BlockSpec((ts, LANES), lambda i: (i, (wq + 6 * wg) // LANES)),
                  tab, tab,
                  pl.BlockSpec((1, LANES), lambda i: (0, 0)),
                  pl.BlockSpec((3, LANES), lambda i: (0, 0)),
                  pl.BlockSpec((4, ts, LANES), lambda i: (0, 0, 0))],
        out_specs=[pl.BlockSpec((ts, wq), lambda i: (i, 0)), grp, grp, grp, grp,
                   wide_spec, wide_spec, narrow_spec, wide_spec, narrow_spec],
        out_shape=[jax.ShapeDtypeStruct((s, wq), BF16), g_bf, g_bf, g_bf, g_bf,
                   wide, wide, narrow, wide, jax.ShapeDtypeStruct((n_groups, s, LANES), F32)],
        compiler_params=_params(("arbitrary",)),
    )(proj, proj, proj, proj, proj, proj, proj, proj, cos_t, sin_t,
      q_gain.reshape(1, LANES), k_gain, pe_tiles)


def _compress_kernel(top_ref, bot_ref, w1_ref, w2_ref, kg_ref, o_ref):
    kv = pl.program_id(0)
    n_half = top_ref.shape[2]
    k_half = top_ref.shape[3]
    h_top = jnp.dot(top_ref[0, 0], w1_ref[0, 0:k_half, :].astype(BF16), preferred_element_type=F32)
    h_bot = jnp.dot(bot_ref[0, 0], w1_ref[0, k_half:2 * k_half, :].astype(BF16), preferred_element_type=F32)
    hid = h_top + pltpu.roll(h_bot, n_half - 1, 0)
    act = hid * jax.nn.sigmoid(hid)
    out = jnp.dot(act.astype(BF16), w2_ref[0].astype(BF16), preferred_element_type=F32)
    ms = jnp.mean(out * out, axis=-1, keepdims=True)
    normed = out * lax.rsqrt(ms + EPS) * kg_ref[...]
    o_ref[0, 0] = jnp.where(kv == 0, normed, out).astype(o_ref.dtype)


def _compress(top, bot, w1, w2, k_gain0):
    _, n_groups, n_half, k_half = top.shape
    hidden = w1.shape[2]
    blk = pl.BlockSpec((1, 1, n_half, k_half), lambda kv, g: (kv, g, 0, 0))
    return pl.pallas_call(
        _compress_kernel,
        grid=(2, n_groups),
        in_specs=[blk, blk,
                  pl.BlockSpec((1, 2 * k_half, hidden), lambda kv, g: (kv, 0, 0)),
                  pl.BlockSpec((1, hidden, HEAD_DIM), lambda kv, g: (kv, 0, 0)),
                  pl.BlockSpec((1, LANES), lambda kv, g: (0, 0))],
        out_specs=pl.BlockSpec((1, 1, n_half, HEAD_DIM), lambda kv, g: (kv, g, 0, 0)),
        out_shape=jax.ShapeDtypeStruct((2, n_groups, n_half, HEAD_DIM), BF16),
        compiler_params=_params(("arbitrary", "arbitrary")),
    )(top, bot, w1, w2, k_gain0.reshape(1, LANES))


def _cmp_select_kernel(q_ref, kc_ref, vc_ref, ov_ref, gate_ref, oc_ref, sel_ref, *, tq, heads_per_group,
                       n_blocks_pad):
    i = pl.program_id(1)
    kc, vc = kc_ref[0, 0], vc_ref[0, 0]
    gates = gate_ref[0]
    n_cmp = kc.shape[0]
    tpos = i * tq + lax.broadcasted_iota(jnp.int32, (tq, n_cmp), 0)
    cidx = lax.broadcasted_iota(jnp.int32, (tq, n_cmp), 1)
    valid = cidx * CMP_STRIDE + (CMP_BLOCK - 1) <= tpos
    p_sum = jnp.zeros((tq, n_cmp), F32)
    for hh in range(heads_per_group):
        sl = slice(hh * LANES, (hh + 1) * LANES)
        s = lax.dot_general(q_ref[:, sl], kc, NT_DIMS, preferred_element_type=F32)
        s = jnp.where(valid, s, NEG)
        e = jnp.where(valid, jnp.exp2(s - jnp.max(s, axis=-1, keepdims=True)), 0.0)
        p = e * (1.0 / jnp.maximum(jnp.sum(e, axis=-1, keepdims=True), TINY))
        oc_ref[:, sl] = gates[:, 3 * hh:3 * hh + 1] * jnp.dot(p.astype(BF16), vc, preferred_element_type=F32)
        p_sum = p_sum + p
    p_hi = p_sum.astype(BF16)
    p_lo = (p_sum - p_hi.astype(F32)).astype(BF16)
    imp = (lax.dot_general(ov_ref[...], p_hi, NT_DIMS, preferred_element_type=F32)
           + lax.dot_general(ov_ref[...], p_lo, NT_DIMS, preferred_element_type=F32))
    blk = lax.broadcasted_iota(jnp.int32, (n_blocks_pad, tq), 0).astype(F32)
    cur = ((i * tq + lax.broadcasted_iota(jnp.int32, (n_blocks_pad, tq), 1)) // SLC_BLOCK).astype(F32)
    forced = (blk == 0.0) | (blk == cur) | (blk == cur - 1.0)
    imp = jnp.where(forced, BIG, imp)
    imp = jnp.where(blk > cur, NEG, imp)
    sel = jnp.zeros((n_blocks_pad, tq), F32)
    for _ in range(SLC_TOPK):
        top = jnp.max(imp, axis=0, keepdims=True)
        first = jnp.min(jnp.where(imp == top, blk, float(n_blocks_pad)), axis=0, keepdims=True)
        hit = blk == first
        sel = jnp.where(hit, 1.0, sel)
        imp = jnp.where(hit, -jnp.inf, imp)
    sel_ref[0] = (sel.T - 1.0).astype(sel_ref.dtype)


def _cmp_select(qn, kcv, gates, n_groups, heads_per_group):
    s = qn.shape[0]
    tq = 512
    n_cmp = kcv.shape[2]
    n_blocks_pad = LANES
    assert s // SLC_BLOCK <= n_blocks_pad
    cmp_start = np.arange(n_cmp) * CMP_STRIDE
    slc_start = np.arange(n_blocks_pad) * SLC_BLOCK
    overlap_t = ((cmp_start[None, :] <= slc_start[:, None] + SLC_BLOCK - 1)
                 & (cmp_start[None, :] + CMP_BLOCK - 1 >= slc_start[:, None]))
    overlap_t = jnp.asarray(overlap_t.astype(np.float32), dtype=BF16)
    wq = heads_per_group * HEAD_DIM
    return pl.pallas_call(
        functools.partial(_cmp_select_kernel, tq=tq, heads_per_group=heads_per_group, n_blocks_pad=n_blocks_pad),
        grid=(n_groups, s // tq),
        in_specs=[pl.BlockSpec((tq, wq), lambda g, i: (i, g)),
                  pl.BlockSpec((1, 1, n_cmp, HEAD_DIM), lambda g, i: (0, g, 0, 0)),
                  pl.BlockSpec((1, 1, n_cmp, HEAD_DIM), lambda g, i: (1, g, 0, 0)),
                  pl.BlockSpec((n_blocks_pad, n_cmp), lambda g, i: (0, 0)),
                  pl.BlockSpec((1, tq, LANES), lambda g, i: (g, i, 0))],
        out_specs=[pl.BlockSpec((tq, wq), lambda g, i: (i, g)),
                   pl.BlockSpec((1, tq, n_blocks_pad), lambda g, i: (g, i, 0))],
        out_shape=[jax.ShapeDtypeStruct(qn.shape, F32),
                   jax.ShapeDtypeStruct((n_groups, s, n_blocks_pad), BF16)],
        compiler_params=_params(("arbitrary", "arbitrary")),
    )(qn, kcv, kcv, overlap_t, gates)


def _slc_attention(qn, sel, ksk, vs, prev, gates, heads_per_group):
    n_groups, s, _ = ksk.shape
    tq, tk, n_group = 256, 256, 4
    wq = heads_per_group * HEAD_DIM
    in_specs = [
        pl.BlockSpec((tq, wq), lambda g, i: (i, g)),
        pl.BlockSpec((1, tq, LANES), lambda g, i: (g, i, 0)),
        pl.BlockSpec((1, s, 2 * LANES), lambda g, i: (g, 0, 0)),
        pl.BlockSpec((1, s, 2 * LANES), lambda g, i: (g, 0, 0)),
        pl.BlockSpec((tq, wq), lambda g, i: (i, g)),
        pl.BlockSpec((1, tq, LANES), lambda g, i: (g, i, 0)),
    ]
    out_spec = pl.BlockSpec((tq, wq), lambda g, i: (i, g))
    rows = heads_per_group * tq
    scratch = [pltpu.VMEM((rows, 2 * LANES), BF16)] + _softmax_scratch(rows, n_group)
    return _attn_call("slc", n_groups, s // tq, in_specs, out_spec, jax.ShapeDtypeStruct(qn.shape, F32),
                      scratch, (qn, sel, ksk, vs, prev, gates), tq=tq, tk=tk, nrep=heads_per_group,
                      n_group=n_group)


def _win_attention(qn, kw, vw, prev, gates, heads_per_group):
    n_groups, s, _ = kw.shape
    tq = tk = 256
    assert 2 * tk == WINDOW
    wq = heads_per_group * HEAD_DIM
    in_specs = [
        pl.BlockSpec((tq, wq), lambda g, i: (i, g)),
        pl.BlockSpec((1, s, LANES), lambda g, i: (g, 0, 0)),
        pl.BlockSpec((1, s, 2 * LANES), lambda g, i: (g, 0, 0)),
        pl.BlockSpec((tq, wq), lambda g, i: (i, g)),
        pl.BlockSpec((1, tq, LANES), lambda g, i: (g, i, 0)),
    ]
    out_spec = pl.BlockSpec((tq, wq), lambda g, i: (i, g))
    rows = heads_per_group * tq
    scratch = [pltpu.VMEM((rows, LANES), BF16)] + _softmax_scratch(rows, n_streams=3)
    return _attn_call("win", n_groups, s // tq, in_specs, out_spec, jax.ShapeDtypeStruct(qn.shape, BF16),
                      scratch, (qn, kw, vw, prev, gates), tq=tq, tk=tk, nrep=heads_per_group)


def _rope_tables(positions, dim):
    inv = ROPE_THETA ** (-jnp.arange(0, dim, 2, dtype=F32) / dim)
    ang = positions.astype(F32)[:, None] * inv
    cos, sin = jnp.cos(ang), jnp.sin(ang)
    reps = LANES // dim
    cos_t = jnp.tile(jnp.concatenate([cos, cos], axis=-1), (1, reps))
    sin_t = jnp.tile(jnp.concatenate([-sin, sin], axis=-1), (1, reps))
    return cos_t, sin_t


def _pad_cols(w, n):
    return jnp.pad(w, ((0, 0), (0, n - w.shape[1])))


def _even_mixer(h, w_in, b_forget, gain_diff, gain_fox, lam_params, subln, tables, lam_init):
    d_model = h.shape[1]
    n_heads = d_model // (2 * HEAD_DIM)
    n_in = 6 * n_heads * HEAD_DIM + 1024
    proj = _matmul(h, _pad_cols(w_in, n_in).astype(BF16), bm=1024, bn=1024, bk=d_model)
    qa, ka, va, qf, kf, vf = _even_prep(proj, *tables, gain_diff, gain_fox, b_forget, n_heads)
    o = _diff_attention(qa, ka, va, lam_params, subln, lam_init, d_model)
    return _fox_attention(qf, kf, vf, o, n_heads)


def _odd_mixer(h, w_in, q_gain, k_gain, cmp_pos, cmp_w1, cmp_w2, tables):
    s, d_model = h.shape
    n_heads, n_groups = d_model // HEAD_DIM, N_KV_GROUPS
    n_in = d_model + 7 * n_groups * HEAD_DIM
    proj = _matmul(h, _pad_cols(w_in, n_in).astype(BF16), bm=1024, bn=n_in // 10, bk=d_model)
    (qn, ktop, kbot, vtop, vbot, ksk, vs, kw, vw, gates) = _odd_prep(
        proj, *tables, q_gain, k_gain, cmp_pos, n_heads, n_groups)

    def halves(k_part, v_part):
        a = jnp.stack([k_part, v_part]).reshape(2, s // CMP_STRIDE, CMP_STRIDE, n_groups, HEAD_DIM)
        return jnp.transpose(a, (0, 3, 1, 2, 4)).reshape(2, n_groups, s // CMP_STRIDE, CMP_STRIDE * HEAD_DIM)

    kcv = _compress(halves(ktop, vtop), halves(kbot, vbot), cmp_w1, cmp_w2, k_gain[0])
    hpg = n_heads // n_groups
    o_cmp, sel = _cmp_select(qn, kcv, gates, n_groups, hpg)
    o_cmp_slc = _slc_attention(qn, sel, ksk, vs, o_cmp, gates, hpg)
    return _win_attention(qn, kw, vw, o_cmp_slc, gates, hpg)


def kernel(x, c, positions, w_ada, b_ada, norm_gain, even_w_in, even_b_forget, even_qk_gain_diff,
           even_qk_gain_fox, even_diff_lambda, even_diff_subln, even_w_out, odd_w_in, odd_q_gain,
           odd_k_gain, odd_cmp_pos, odd_cmp_w1, odd_cmp_w2, odd_w_out, mlp_w1, mlp_w2):
    batch, s, d_model = x.shape
    assert batch == 1
    depth = w_ada.shape[0]
    xs = x[0]
    pos = positions[0]
    tables64 = _rope_tables(pos, DIFF_QK_DIM)
    tables128 = _rope_tables(pos, HEAD_DIM)
    mod = _adaln(c, w_ada, b_ada)
    for i in range(depth):
        sh1, sc1, g1, sh2, sc2, g2 = [mod[i, :, k * d_model:(k + 1) * d_model] for k in range(6)]
        h = _norm_mod(xs, norm_gain[i, 0], sc1, sh1)
        if i % 2 == 0:
            e = i // 2
            lam_init = 0.8 - 0.6 * math.exp(-0.3 * i)
            o = _even_mixer(h, even_w_in[e], even_b_forget[e], even_qk_gain_diff[e], even_qk_gain_fox[e],
                            even_diff_lambda[e], even_diff_subln[e], tables64, lam_init)
            w_out = even_w_out[e]
        else:
            od = i // 2
            o = _odd_mixer(h, odd_w_in[od], odd_q_gain[od], odd_k_gain[od], odd_cmp_pos[od],
                           odd_cmp_w1[od], odd_cmp_w2[od], tables128)
            w_out = odd_w_out[od]
        xs = _matmul(o, w_out.astype(BF16), bm=512, bn=1024, bk=d_model, epilogue="resid", resid=xs, gate=g1)
        h = _norm_mod(xs, norm_gain[i, 1], sc2, sh2)
        ff = _matmul(h, mlp_w1[i].astype(BF16), bm=1024, bn=1024, bk=d_model, epilogue="relu2", out_dtype=BF16)
        xs = _matmul(ff, mlp_w2[i].astype(BF16), bm=1024, bn=1024, bk=2048, epilogue="resid", resid=xs, gate=g2)
    return xs[None]
```

```python
import functools
import math

import numpy as np
import jax
import jax.numpy as jnp
from jax import lax
from jax.experimental import pallas as pl
from jax.experimental.pallas import tpu as pltpu

F32 = jnp.float32
BF16 = jnp.bfloat16

HEAD_DIM = 128
DIFF_QK_DIM = 64
N_KV_GROUPS = 4
CMP_BLOCK = 32
CMP_STRIDE = 16
SLC_BLOCK = 64
SLC_TOPK = 16
WINDOW = 512
ROPE_THETA = 10000.0
EPS = 1e-6
NEG = -1e30
BIG = 1e30
TINY = 1e-20
UNSELECTED_SCORE = 2.0 ** 99
LOG2E = 1.4426950408889634
LANES = 128
VMEM_LIMIT = 56 * 1024 * 1024

NT_DIMS = (((1,), (1,)), ((), ()))


def _params(sem):
    return pltpu.CompilerParams(dimension_semantics=sem, vmem_limit_bytes=VMEM_LIMIT)


def _adaln_kernel(cb_ref, w_ref, b_ref, o_ref):
    d_model, tn = w_ref.shape[1], w_ref.shape[2]
    rows_per_step = 256

    def body(r, acc):
        rows = pl.ds(pl.multiple_of(r * rows_per_step, rows_per_step), rows_per_step)
        cb = cb_ref[rows, :]
        cond = cb * jax.nn.sigmoid(cb)
        prod = w_ref[0, rows, :] * jnp.concatenate([cond] * (tn // LANES), axis=1)
        return acc + prod.reshape(rows_per_step // 8, 8, tn).sum(axis=0)

    acc = lax.fori_loop(0, d_model // rows_per_step, body, jnp.zeros((8, tn), F32))
    o_ref[0] = acc.sum(axis=0, keepdims=True) + b_ref[0]


def _adaln(c, w_ada, b_ada):
    depth, d_model, n_out = w_ada.shape
    tn = 512
    cb = jnp.broadcast_to(c.reshape(d_model, 1), (d_model, LANES))
    return pl.pallas_call(
        _adaln_kernel,
        grid=(depth, n_out // tn),
        in_specs=[
            pl.BlockSpec((d_model, LANES), lambda l, n: (0, 0)),
            pl.BlockSpec((1, d_model, tn), lambda l, n: (l, 0, n)),
            pl.BlockSpec((1, 1, tn), lambda l, n: (l, 0, n)),
        ],
        out_specs=pl.BlockSpec((1, 1, tn), lambda l, n: (l, 0, n)),
        out_shape=jax.ShapeDtypeStruct((depth, 1, n_out), F32),
        compiler_params=_params(("arbitrary", "arbitrary")),
    )(cb, w_ada, b_ada.reshape(depth, 1, n_out))


def _norm_mod_kernel(x_ref, g_ref, sc_ref, sh_ref, o_ref):
    x = x_ref[...]
    ms = jnp.mean(x * x, axis=-1, keepdims=True)
    y = x * lax.rsqrt(ms + EPS) * g_ref[...]
    o_ref[...] = (y * (1.0 + sc_ref[...]) + sh_ref[...]).astype(o_ref.dtype)


def _norm_mod(x, gain, scale, shift):
    s, d_model = x.shape
    ts = 256
    row = pl.BlockSpec((1, d_model), lambda i: (0, 0))
    return pl.pallas_call(
        _norm_mod_kernel,
        grid=(s // ts,),
        in_specs=[pl.BlockSpec((ts, d_model), lambda i: (i, 0)), row, row, row],
        out_specs=pl.BlockSpec((ts, d_model), lambda i: (i, 0)),
        out_shape=jax.ShapeDtypeStruct((s, d_model), BF16),
        compiler_params=_params(("arbitrary",)),
    )(x, gain.reshape(1, d_model), scale, shift)


def _mm_kernel(*refs, nk, epilogue):
    if epilogue == "resid":
        a_ref, b_ref, x_ref, g_ref, o_ref = refs[:5]
    else:
        a_ref, b_ref, o_ref = refs[:3]

    def finish(acc):
        if epilogue == "plain":
            o_ref[...] = acc.astype(o_ref.dtype)
        elif epilogue == "relu2":
            r = jnp.maximum(acc, 0.0)
            o_ref[...] = (r * r).astype(o_ref.dtype)
        else:
            o_ref[...] = x_ref[...] + g_ref[...] * acc

    part = jnp.dot(a_ref[...], b_ref[...], preferred_element_type=F32)
    if nk == 1:
        finish(part)
    else:
        acc_ref = refs[-1]
        k = pl.program_id(2)

        @pl.when(k == 0)
        def _():
            acc_ref[...] = part

        @pl.when(k > 0)
        def _():
            acc_ref[...] += part

        @pl.when(k == nk - 1)
        def _():
            finish(acc_ref[...])


def _matmul(a, b, *, bm, bn, bk, epilogue="plain", out_dtype=F32, resid=None, gate=None):
    m, kdim = a.shape
    n = b.shape[1]
    nk = kdim // bk
    in_specs = [pl.BlockSpec((bm, bk), lambda i, j, k: (i, k)),
                pl.BlockSpec((bk, bn), lambda i, j, k: (k, j))]
    args = [a, b]
    if epilogue == "resid":
        in_specs += [pl.BlockSpec((bm, bn), lambda i, j, k: (i, j)),
                     pl.BlockSpec((1, bn), lambda i, j, k: (0, j))]
        args += [resid, gate]
    scratch = [pltpu.VMEM((bm, bn), F32)] if nk > 1 else []
    return pl.pallas_call(
        functools.partial(_mm_kernel, nk=nk, epilogue=epilogue),
        grid=(m // bm, n // bn, nk),
        in_specs=in_specs,
        out_specs=pl.BlockSpec((bm, bn), lambda i, j, k: (i, j)),
        out_shape=jax.ShapeDtypeStruct((m, n), out_dtype),
        scratch_shapes=scratch,
        compiler_params=_params(("arbitrary", "arbitrary", "arbitrary")),
    )(*args)


def _mm_resident_kernel(*refs, n_a, epilogue):
    a_refs, b_ref = refs[:n_a], refs[n_a]
    if epilogue == "resid":
        x_ref, g_ref, o_ref, b_bf_ref = refs[n_a + 1:]
    else:
        o_ref, b_bf_ref = refs[n_a + 1:]

    @pl.when(pl.program_id(1) == 0)
    def _():
        b_bf_ref[...] = b_ref[0].astype(BF16)

    k_part = b_bf_ref.shape[0] // n_a
    acc = jnp.dot(a_refs[0][...], b_bf_ref[0:k_part, :], preferred_element_type=F32)
    for t in range(1, n_a):
        acc = acc + jnp.dot(a_refs[t][...], b_bf_ref[t * k_part:(t + 1) * k_part, :], preferred_element_type=F32)
    if epilogue == "relu2":
        r = jnp.maximum(acc, 0.0)
        o_ref[...] = (r * r).astype(o_ref.dtype)
    else:
        o_ref[...] = x_ref[...] + g_ref[...] * acc


def _matmul_resident(a_parts, w, layer, *, bm, bn, epilogue, out_dtype, resid=None, gate=None):
    m = a_parts[0].shape[0]
    _, kdim, n = w.shape
    n_a = len(a_parts)
    in_specs = [pl.BlockSpec((bm, kdim // n_a), lambda j, i: (i, 0)) for _ in a_parts]
    in_specs.append(pl.BlockSpec((1, kdim, bn), lambda j, i: (layer, 0, j)))
    args = list(a_parts) + [w]
    if epilogue == "resid":
        in_specs += [pl.BlockSpec((bm, bn), lambda j, i: (i, j)), pl.BlockSpec((1, bn), lambda j, i: (0, j))]
        args += [resid, gate]
    return pl.pallas_call(
        functools.partial(_mm_resident_kernel, n_a=n_a, epilogue=epilogue),
        grid=(n // bn, m // bm),
        in_specs=in_specs,
        out_specs=pl.BlockSpec((bm, bn), lambda j, i: (i, j)),
        out_shape=jax.ShapeDtypeStruct((m, n), out_dtype),
        scratch_shapes=[pltpu.VMEM((kdim, bn), BF16)],
        compiler_params=_params(("arbitrary", "arbitrary")),
    )(*args)


def _rms(x, n_chunk, gain):
    sq = x * x
    if n_chunk == LANES:
        ss = jnp.sum(sq, axis=-1, keepdims=True)
    else:
        lane = lax.broadcasted_iota(jnp.int32, x.shape, 1)
        low = lane < n_chunk
        s_low = jnp.sum(jnp.where(low, sq, 0.0), axis=-1, keepdims=True)
        s_all = jnp.sum(sq, axis=-1, keepdims=True)
        ss = jnp.where(low, s_low, s_all - s_low)
    return x * lax.rsqrt(ss * (1.0 / n_chunk) + EPS) * gain


def _rope(x, cos_t, sin_t, n_chunk):
    half = n_chunk // 2
    if n_chunk == LANES:
        partner = pltpu.roll(x, half, 1)
    else:
        lane = lax.broadcasted_iota(jnp.int32, x.shape, 1)
        first = (lane % n_chunk) < half
        partner = jnp.where(first, pltpu.roll(x, LANES - half, 1), pltpu.roll(x, half, 1))
    return x * cos_t + partner * sin_t


def _split3(v):
    hi = v.astype(BF16)
    r1 = v - hi.astype(F32)
    mid = r1.astype(BF16)
    lo = (r1 - mid.astype(F32)).astype(BF16)
    return hi, mid, lo


def _even_prep_kernel(qa_ref, ka_ref, va_ref, qb_ref, kb_ref, vb_ref, fb_ref, cos_ref, sin_ref,
                      gdq_ref, gdk_ref, gfq_ref, gfk_ref, bf_ref,
                      qa_o, ka_o, va_o, qf_o, kf_o, vf_o, carry_ref, *, n_heads):
    i = pl.program_id(0)
    ts = qa_ref.shape[0]
    cos_t, sin_t = cos_ref[...], sin_ref[...]
    lane = lax.broadcasted_iota(jnp.int32, (ts, LANES), 1)
    ones_aug = jnp.where(lane < 3, 1.0, 0.0).astype(BF16)
    ones = jnp.ones((ts, LANES), BF16)

    z = fb_ref[...] + bf_ref[...]
    logf = jnp.minimum(z, 0.0) - jnp.log(1.0 + jnp.exp(-jnp.abs(z)))
    r = lax.broadcasted_iota(jnp.int32, (ts, ts), 0)
    c = lax.broadcasted_iota(jnp.int32, (ts, ts), 1)
    tri = jnp.where(c <= r, 1.0, 0.0).astype(BF16)

    @pl.when(i == 0)
    def _():
        carry_ref[...] = jnp.zeros_like(carry_ref)

    f_cum = carry_ref[0:1, :]
    for part in _split3(logf):
        f_cum = f_cum + jnp.dot(tri, part, preferred_element_type=F32)
    carry_ref[...] = jnp.broadcast_to(f_cum[ts - 1:ts, :], carry_ref.shape)
    neg_hi, neg_mid, neg_lo = _split3(f_cum * (-LOG2E))
    prow = lax.broadcasted_iota(jnp.int32, (LANES, LANES), 0)
    pcol = lax.broadcasted_iota(jnp.int32, (LANES, LANES), 1)

    for h in range(n_heads):
        sl = slice(h * LANES, (h + 1) * LANES)
        qa = _rope(_rms(qa_ref[:, sl], DIFF_QK_DIM, gdq_ref[...]), cos_t, sin_t, DIFF_QK_DIM)
        qa_o[h] = (qa * (DIFF_QK_DIM ** -0.5 * LOG2E)).astype(BF16)
        ka = _rope(_rms(ka_ref[:, sl], DIFF_QK_DIM, gdk_ref[...]), cos_t, sin_t, DIFF_QK_DIM)
        ka_o[h] = ka.astype(BF16)
        va_o[h, :, 0:LANES] = va_ref[:, sl].astype(BF16)
        va_o[h, :, LANES:2 * LANES] = ones
        qb = _rms(qb_ref[:, sl], HEAD_DIM, gfq_ref[...]) * (HEAD_DIM ** -0.5 * LOG2E)
        qf_o[h, :, 0:LANES] = qb.astype(BF16)
        qf_o[h, :, LANES:2 * LANES] = ones_aug
        kb = _rms(kb_ref[:, sl], HEAD_DIM, gfk_ref[...])
        kf_o[h, :, 0:LANES] = kb.astype(BF16)
        aug = jnp.zeros((ts, LANES), F32)
        for lane_idx, part in enumerate((neg_hi, neg_mid, neg_lo)):
            place = jnp.where((prow == h) & (pcol == lane_idx), 1.0, 0.0).astype(BF16)
            aug = aug + jnp.dot(part, place, preferred_element_type=F32)
        kf_o[h, :, LANES:2 * LANES] = aug.astype(BF16)
        vf_o[h, :, 0:LANES] = vb_ref[:, sl].astype(BF16)
        vf_o[h, :, LANES:2 * LANES] = ones


def _even_prep(proj, cos_t, sin_t, gain_diff, gain_fox, b_forget, n_heads):
    s = proj.shape[0]
    ts = 256
    w = n_heads * HEAD_DIM
    seg = lambda k: pl.BlockSpec((ts, w), lambda i, k=k: (i, k))
    tab = pl.BlockSpec((ts, LANES), lambda i: (i, 0))
    row = pl.BlockSpec((1, LANES), lambda i: (0, 0))
    gdq = jnp.tile(gain_diff[0], 2).reshape(1, LANES)
    gdk = jnp.tile(gain_diff[1], 2).reshape(1, LANES)
    bfp = jnp.zeros((1, LANES), F32).at[0, :n_heads].set(b_forget)
    narrow = jax.ShapeDtypeStruct((n_heads, s, LANES), BF16)
    wide = jax.ShapeDtypeStruct((n_heads, s, 2 * LANES), BF16)
    narrow_spec = pl.BlockSpec((n_heads, ts, LANES), lambda i: (0, i, 0))
    wide_spec = pl.BlockSpec((n_heads, ts, 2 * LANES), lambda i: (0, i, 0))
    return pl.pallas_call(
        functools.partial(_even_prep_kernel, n_heads=n_heads),
        grid=(s // ts,),
        in_specs=[seg(0), seg(1), seg(2), seg(3), seg(4), seg(5),
                  pl.BlockSpec((ts, LANES), lambda i: (i, 6 * w // LANES)),
                  tab, tab, row, row, row, row, row],
        out_specs=[narrow_spec, narrow_spec, wide_spec, wide_spec, wide_spec, wide_spec],
        out_shape=[narrow, narrow, wide, wide, wide, wide],
        scratch_shapes=[pltpu.VMEM((8, LANES), F32)],
        compiler_params=_params(("arbitrary",)),
    )(proj, proj, proj, proj, proj, proj, proj, cos_t, sin_t, gdq, gdk,
      gain_fox[0].reshape(1, LANES), gain_fox[1].reshape(1, LANES), bfp)


MASK_DIAG, MASK_EDGE = 0, 1


def _attn_kernel(*refs, mode, tq, tk, nrep, group_sizes, lam_init):
    bias_ref, refs = refs[0], refs[1:]
    if mode == "diff":
        q_ref, k_ref, v_ref, lam_ref, sub_ref, o_ref, qs_ref, m_ref, acc_ref = refs
    elif mode == "fox":
        q_ref, k_ref, v_ref, o_ref, m_ref, acc_ref = refs
    elif mode == "slc":
        q_ref, sel_ref, k_ref, v_ref, prev_ref, gate_ref, o_ref, qs_ref, m_ref, acc_ref = refs
    else:
        q_ref, k_ref, v_ref, prev_ref, gate_ref, o_ref, qs_ref, m_ref, acc_ref = refs
    i = pl.program_id(1)
    rows = nrep * tq
    q_lo = i * tq
    blk = min(tq, tk)
    n_blk = tq // blk

    def row0(b, rep):
        return (b * nrep + rep) * blk

    for b in range(n_blk):
        qrows = slice(b * blk, (b + 1) * blk)
        if mode == "diff":
            q = q_ref[0, qrows, :]
            lane = lax.broadcasted_iota(jnp.int32, q.shape, 1)
            qs_ref[row0(b, 0):row0(b, 0) + blk, :] = jnp.where(lane < DIFF_QK_DIM, q, jnp.zeros_like(q))
            qs_ref[row0(b, 1):row0(b, 1) + blk, :] = jnp.where(lane >= DIFF_QK_DIM, q, jnp.zeros_like(q))
        elif mode in ("slc", "win"):
            for hh in range(nrep):
                dst = slice(row0(b, hh), row0(b, hh) + blk)
                qs_ref[dst, 0:LANES] = q_ref[qrows, hh * LANES:(hh + 1) * LANES]
                if mode == "slc":
                    qs_ref[dst, LANES:2 * LANES] = sel_ref[0, qrows, :]
    m_ref[...] = jnp.full(m_ref.shape, NEG, F32)
    acc_ref[...] = jnp.zeros(acc_ref.shape, F32)
    n_streams = m_ref.shape[0]

    def process(chunks):
        for c, masked, st, r0 in chunks:
            nrows = rows - r0
            k0 = pl.multiple_of(c * tk, tk)
            k = k_ref[0, pl.ds(k0, tk), :]
            v = v_ref[0, pl.ds(k0, tk), :]
            q_all = q_ref[0, r0:rows, :] if mode == "fox" else qs_ref[r0:rows, :]
            s = lax.dot_general(q_all, k, NT_DIMS, preferred_element_type=F32)
            if masked is not None:
                nb = nrep * blk
                top = s[0:nb] + bias_ref[masked]
                s = top if nb == nrows else jnp.concatenate([top, s[nb:nrows]], axis=0)
            tiles = [s[:, t * LANES:(t + 1) * LANES] for t in range(tk // LANES)]
            tile_max = tiles[0]
            for t in tiles[1:]:
                tile_max = jnp.maximum(tile_max, t)
            m_prev = m_ref[st, r0:rows, :]
            m_new = jnp.maximum(m_prev, jnp.max(tile_max, axis=-1, keepdims=True))
            alpha = jnp.exp2(m_prev - m_new)
            p = jnp.concatenate([jnp.exp2(t - m_new) for t in tiles], axis=1).astype(BF16)
            pv = jnp.dot(p, v, preferred_element_type=F32)
            acc_ref[st, r0:rows, 0:LANES] = alpha * acc_ref[st, r0:rows, 0:LANES] + pv[:, 0:LANES]
            acc_ref[st, r0:rows, LANES:2 * LANES] = (alpha * acc_ref[st, r0:rows, LANES:2 * LANES]
                                                     + pv[:, LANES:2 * LANES])
            m_ref[st, r0:rows, :] = m_new

    if mode == "win":
        @pl.when(i >= 2)
        def _():
            process([(i - 2, MASK_EDGE, 0, 0), (i - 1, None, 1, 0), (i, MASK_DIAG, 2, 0)])

        @pl.when(i == 1)
        def _():
            process([(i - 1, None, 1, 0), (i, MASK_DIAG, 2, 0)])

        @pl.when(i == 0)
        def _():
            process([(i, MASK_DIAG, 2, 0)])
    else:
        n_full = q_lo // tk
        start = 0
        for n in group_sizes:
            count = (n_full - start) // n

            def group_body(g, carry, n=n, start=start):
                process([(start + g * n + j, None, j, 0) for j in range(n)])
                return carry

            lax.fori_loop(0, count, group_body, 0)
            start = start + count * n
        process([(n_full + j, MASK_DIAG, j, row0(j, 0)) for j in range(n_blk)])

    m_all = m_ref[0]
    for st in range(1, n_streams):
        m_all = jnp.maximum(m_all, m_ref[st])
    num = jnp.zeros((rows, LANES), F32)
    den = jnp.zeros((rows, LANES), F32)
    for st in range(n_streams):
        w = jnp.exp2(m_ref[st] - m_all)
        num = num + w * acc_ref[st, :, 0:LANES]
        den = den + w * acc_ref[st, :, LANES:2 * LANES]
    o = num / den
    if mode == "diff":
        lp = lam_ref[...]
        lam = (jnp.exp(jnp.sum(lp[0:1] * lp[1:2], axis=-1, keepdims=True))
               - jnp.exp(jnp.sum(lp[2:3] * lp[3:4], axis=-1, keepdims=True)) + lam_init)
    for b in range(n_blk):
        qrows = slice(b * blk, (b + 1) * blk)
        if mode == "diff":
            od = o[row0(b, 0):row0(b, 0) + blk] - lam * o[row0(b, 1):row0(b, 1) + blk]
            ms = jnp.mean(od * od, axis=-1, keepdims=True)
            o_ref[qrows, :] = (od * lax.rsqrt(ms + EPS) * sub_ref[...] * (1.0 - lam_init)).astype(o_ref.dtype)
        elif mode == "fox":
            o_ref[qrows, :] = o[qrows].astype(o_ref.dtype)
        else:
            branch = 1 if mode == "slc" else 2
            gates = gate_ref[0, qrows, :]
            for hh in range(nrep):
                cols = slice(hh * LANES, (hh + 1) * LANES)
                gate = gates[:, 3 * hh + branch:3 * hh + branch + 1]
                o_ref[qrows, cols] = (prev_ref[qrows, cols]
                                      + gate * o[row0(b, hh):row0(b, hh) + blk]).astype(o_ref.dtype)


def _mask_bias(nrep, tk, with_edge):
    row = np.tile(np.arange(tk), nrep)[:, None]
    col = np.arange(tk)[None, :]
    diag = np.where(col <= row, 0.0, NEG).astype(np.float32)
    kinds = [diag, np.where(col <= row, NEG, 0.0).astype(np.float32)] if with_edge else [diag]
    return jnp.asarray(np.stack(kinds))


def _attn_call(mode, n_outer, n_q, in_specs, out_spec, out_shape, qs_width, args, *, tq, tk, nrep,
               group_sizes=(), lam_init=0.0):
    assert tq % tk == 0
    rows = nrep * tq
    n_streams = max(max(group_sizes, default=1), tq // tk, 3 if mode == "win" else 1)
    bias = _mask_bias(nrep, tk, with_edge=(mode == "win"))
    scratch = [] if qs_width is None else [pltpu.VMEM((rows, qs_width), BF16)]
    scratch += [pltpu.VMEM((n_streams, rows, LANES), F32), pltpu.VMEM((n_streams, rows, 2 * LANES), F32)]
    bias_spec = pl.BlockSpec(bias.shape, lambda h, i: (0, 0, 0))
    return pl.pallas_call(
        functools.partial(_attn_kernel, mode=mode, tq=tq, tk=tk, nrep=nrep, group_sizes=group_sizes,
                          lam_init=lam_init),
        grid=(n_outer, n_q), in_specs=[bias_spec] + in_specs, out_specs=out_spec, out_shape=out_shape,
        scratch_shapes=scratch,
        compiler_params=_params(("arbitrary", "arbitrary")),
    )(bias, *args)


def _diff_attention(qa, ka, va, lam_params, subln, lam_init):
    n_heads, s, _ = qa.shape
    tq, tk = 1024, 256
    in_specs = [
        pl.BlockSpec((1, tq, LANES), lambda h, i: (h, i, 0)),
        pl.BlockSpec((1, s, LANES), lambda h, i: (h, 0, 0)),
        pl.BlockSpec((1, s, 2 * LANES), lambda h, i: (h, 0, 0)),
        pl.BlockSpec((4, DIFF_QK_DIM), lambda h, i: (0, 0)),
        pl.BlockSpec((1, LANES), lambda h, i: (0, 0)),
    ]
    out_spec = pl.BlockSpec((tq, LANES), lambda h, i: (i, h))
    return _attn_call("diff", n_heads, s // tq, in_specs, out_spec,
                      jax.ShapeDtypeStruct((s, n_heads * HEAD_DIM), BF16), LANES,
                      (qa, ka, va, lam_params, subln.reshape(1, LANES)),
                      tq=tq, tk=tk, nrep=2, group_sizes=(4,), lam_init=lam_init)


def _fox_attention(qf, kf, vf):
    n_heads, s, _ = qf.shape
    tq, tk = 1024, 256
    in_specs = [
        pl.BlockSpec((1, tq, 2 * LANES), lambda h, i: (h, i, 0)),
        pl.BlockSpec((1, s, 2 * LANES), lambda h, i: (h, 0, 0)),
        pl.BlockSpec((1, s, 2 * LANES), lambda h, i: (h, 0, 0)),
    ]
    out_spec = pl.BlockSpec((tq, LANES), lambda h, i: (i, h))
    return _attn_call("fox", n_heads, s // tq, in_specs, out_spec,
                      jax.ShapeDtypeStruct((s, n_heads * HEAD_DIM), BF16), None,
                      (qf, kf, vf), tq=tq, tk=tk, nrep=1, group_sizes=(8, 4))


def _odd_prep_kernel(q_ref, kc_ref, vc_ref, ks_ref, vs_ref, kw_ref, vw_ref, gl_ref, cos_ref, sin_ref,
                     qg_ref, kg_ref, pe_ref,
                     qn_o, ktop_o, kbot_o, vtop_o, vbot_o, ksk_o, vs_o, kw_o, vw_o, gates_o,
                     *, n_heads, n_groups):
    i = pl.program_id(0)
    ts = q_ref.shape[0]
    cos_t, sin_t = cos_ref[...], sin_ref[...]
    ones = jnp.ones((ts, LANES), BF16)
    for h in range(n_heads):
        sl = slice(h * LANES, (h + 1) * LANES)
        q = _rope(_rms(q_ref[:, sl], HEAD_DIM, qg_ref[...]), cos_t, sin_t, HEAD_DIM)
        qn_o[:, sl] = (q * (HEAD_DIM ** -0.5 * LOG2E)).astype(BF16)
    key_block = (i * ts + lax.broadcasted_iota(jnp.int32, (ts, LANES), 0)) // SLC_BLOCK
    lane = lax.broadcasted_iota(jnp.int32, (ts, LANES), 1)
    block_onehot = jnp.where(lane == key_block, UNSELECTED_SCORE, 0.0).astype(BF16)
    for g in range(n_groups):
        sl = slice(g * LANES, (g + 1) * LANES)
        kc = _rope(kc_ref[:, sl], cos_t, sin_t, HEAD_DIM)
        ktop_o[:, sl] = (kc + pe_ref[0]).astype(BF16)
        kbot_o[:, sl] = (kc + pe_ref[1]).astype(BF16)
        vc = vc_ref[:, sl]
        vtop_o[:, sl] = (vc + pe_ref[2]).astype(BF16)
        vbot_o[:, sl] = (vc + pe_ref[3]).astype(BF16)
        ks = _rope(_rms(ks_ref[:, sl], HEAD_DIM, kg_ref[1:2, :]), cos_t, sin_t, HEAD_DIM)
        ksk_o[g, :, 0:LANES] = ks.astype(BF16)
        ksk_o[g, :, LANES:2 * LANES] = block_onehot
        kw = _rope(_rms(kw_ref[:, sl], HEAD_DIM, kg_ref[2:3, :]), cos_t, sin_t, HEAD_DIM)
        kw_o[g] = kw.astype(BF16)
        vs_o[g, :, 0:LANES] = vs_ref[:, sl].astype(BF16)
        vs_o[g, :, LANES:2 * LANES] = ones
        vw_o[g, :, 0:LANES] = vw_ref[:, sl].astype(BF16)
        vw_o[g, :, LANES:2 * LANES] = ones
    gates = jax.nn.sigmoid(gl_ref[...])
    lanes_per_group = 3 * n_heads // n_groups
    for g in range(n_groups):
        gates_o[g] = gates if g == 0 else pltpu.roll(gates, LANES - g * lanes_per_group, 1)


def _odd_prep(proj, cos_t, sin_t, q_gain, k_gain, cmp_pos, n_heads, n_groups):
    s = proj.shape[0]
    ts = 256
    wq, wg = n_heads * HEAD_DIM, n_groups * HEAD_DIM
    seg = lambda k: pl.BlockSpec((ts, wg), lambda i, k=k: (i, wq // wg + k))
    tab = pl.BlockSpec((ts, LANES), lambda i: (i, 0))
    grp = pl.BlockSpec((ts, wg), lambda i: (i, 0))
    narrow_spec = pl.BlockSpec((n_groups, ts, LANES), lambda i: (0, i, 0))
    wide_spec = pl.BlockSpec((n_groups, ts, 2 * LANES), lambda i: (0, i, 0))
    pe_tiles = jnp.stack([jnp.tile(cmp_pos[kv, half * CMP_STRIDE:(half + 1) * CMP_STRIDE], (ts // CMP_STRIDE, 1))
                          for kv in range(2) for half in range(2)])
    g_bf = jax.ShapeDtypeStruct((s, wg), BF16)
    narrow = jax.ShapeDtypeStruct((n_groups, s, LANES), BF16)
    wide = jax.ShapeDtypeStruct((n_groups, s, 2 * LANES), BF16)
    return pl.pallas_call(
        functools.partial(_odd_prep_kernel, n_heads=n_heads, n_groups=n_groups),
        grid=(s // ts,),
        in_specs=[pl.BlockSpec((ts, wq), lambda i: (i, 0)), seg(0), seg(1), seg(2), seg(3), seg(4), seg(5),
                  pl.BlockSpec((ts, LANES), lambda i: (i, (wq + 6 * wg) // LANES)),
                  tab, tab,
                  pl.BlockSpec((1, LANES), lambda i: (0, 0)),
                  pl.BlockSpec((3, LANES), lambda i: (0, 0)),
                  pl.BlockSpec((4, ts, LANES), lambda i: (0, 0, 0))],
        out_specs=[pl.BlockSpec((ts, wq), lambda i: (i, 0)), grp, grp, grp, grp,
                   wide_spec, wide_spec, narrow_spec, wide_spec, narrow_spec],
        out_shape=[jax.ShapeDtypeStruct((s, wq), BF16), g_bf, g_bf, g_bf, g_bf,
                   wide, wide, narrow, wide, jax.ShapeDtypeStruct((n_groups, s, LANES), F32)],
        compiler_params=_params(("arbitrary",)),
    )(proj, proj, proj, proj, proj, proj, proj, proj, cos_t, sin_t,
      q_gain.reshape(1, LANES), k_gain, pe_tiles)


def _compress_kernel(top_ref, bot_ref, w1_ref, w2_ref, kg_ref, o_ref):
    kv = pl.program_id(0)
    n_half = top_ref.shape[2]
    k_half = top_ref.shape[3]
    h_top = jnp.dot(top_ref[0, 0], w1_ref[0, 0:k_half, :].astype(BF16), preferred_element_type=F32)
    h_bot = jnp.dot(bot_ref[0, 0], w1_ref[0, k_half:2 * k_half, :].astype(BF16), preferred_element_type=F32)
    hid = h_top + pltpu.roll(h_bot, n_half - 1, 0)
    act = hid * jax.nn.sigmoid(hid)
    out = jnp.dot(act.astype(BF16), w2_ref[0].astype(BF16), preferred_element_type=F32)
    ms = jnp.mean(out * out, axis=-1, keepdims=True)
    normed = out * lax.rsqrt(ms + EPS) * kg_ref[...]
    o_ref[0, 0] = jnp.where(kv == 0, normed, out).astype(o_ref.dtype)


def _compress(top, bot, w1, w2, k_gain0):
    _, n_groups, n_half, k_half = top.shape
    hidden = w1.shape[2]
    blk = pl.BlockSpec((1, 1, n_half, k_half), lambda kv, g: (kv, g, 0, 0))
    return pl.pallas_call(
        _compress_kernel,
        grid=(2, n_groups),
        in_specs=[blk, blk,
                  pl.BlockSpec((1, 2 * k_half, hidden), lambda kv, g: (kv, 0, 0)),
                  pl.BlockSpec((1, hidden, HEAD_DIM), lambda kv, g: (kv, 0, 0)),
                  pl.BlockSpec((1, LANES), lambda kv, g: (0, 0))],
        out_specs=pl.BlockSpec((1, 1, n_half, HEAD_DIM), lambda kv, g: (kv, g, 0, 0)),
        out_shape=jax.ShapeDtypeStruct((2, n_groups, n_half, HEAD_DIM), BF16),
        compiler_params=_params(("arbitrary", "arbitrary")),
    )(top, bot, w1, w2, k_gain0.reshape(1, LANES))


def _cmp_select_kernel(q_ref, kc_ref, vc_ref, ov_ref, gate_ref, oc_ref, sel_ref, *, tq, heads_per_group,
                       n_blocks_pad):
    i = pl.program_id(1)
    kc, vc = kc_ref[0, 0], vc_ref[0, 0]
    gates = gate_ref[0]
    n_cmp = kc.shape[0]
    tpos = i * tq + lax.broadcasted_iota(jnp.int32, (tq, n_cmp), 0)
    cidx = lax.broadcasted_iota(jnp.int32, (tq, n_cmp), 1)
    valid = cidx * CMP_STRIDE + (CMP_BLOCK - 1) <= tpos
    p_sum = jnp.zeros((tq, n_cmp), F32)
    for hh in range(heads_per_group):
        sl = slice(hh * LANES, (hh + 1) * LANES)
        s = lax.dot_general(q_ref[:, sl], kc, NT_DIMS, preferred_element_type=F32)
        s = jnp.where(valid, s, NEG)
        e = jnp.where(valid, jnp.exp2(s - jnp.max(s, axis=-1, keepdims=True)), 0.0)
        p = e * (1.0 / jnp.maximum(jnp.sum(e, axis=-1, keepdims=True), TINY))
        oc_ref[:, sl] = gates[:, 3 * hh:3 * hh + 1] * jnp.dot(p.astype(BF16), vc, preferred_element_type=F32)
        p_sum = p_sum + p
    p_hi = p_sum.astype(BF16)
    p_lo = (p_sum - p_hi.astype(F32)).astype(BF16)
    imp = (lax.dot_general(ov_ref[...], p_hi, NT_DIMS, preferred_element_type=F32)
           + lax.dot_general(ov_ref[...], p_lo, NT_DIMS, preferred_element_type=F32))
    blk = lax.broadcasted_iota(jnp.int32, (n_blocks_pad, tq), 0).astype(F32)
    cur = ((i * tq + lax.broadcasted_iota(jnp.int32, (n_blocks_pad, tq), 1)) // SLC_BLOCK).astype(F32)
    forced = (blk == 0.0) | (blk == cur) | (blk == cur - 1.0)
    imp = jnp.where(forced, BIG, imp)
    imp = jnp.where(blk > cur, NEG, imp)
    sel = jnp.zeros((n_blocks_pad, tq), F32)
    for _ in range(SLC_TOPK):
        top = jnp.max(imp, axis=0, keepdims=True)
        first = jnp.min(jnp.where(imp == top, blk, float(n_blocks_pad)), axis=0, keepdims=True)
        hit = blk == first
        sel = jnp.where(hit, 1.0, sel)
        imp = jnp.where(hit, -jnp.inf, imp)
    sel_ref[0] = (sel.T - 1.0).astype(sel_ref.dtype)


def _cmp_select(qn, kcv, gates, n_groups, heads_per_group):
    s = qn.shape[0]
    tq = 512
    n_cmp = kcv.shape[2]
    n_blocks_pad = LANES
    assert s // SLC_BLOCK <= n_blocks_pad
    cmp_start = np.arange(n_cmp) * CMP_STRIDE
    slc_start = np.arange(n_blocks_pad) * SLC_BLOCK
    overlap_t = ((cmp_start[None, :] <= slc_start[:, None] + SLC_BLOCK - 1)
                 & (cmp_start[None, :] + CMP_BLOCK - 1 >= slc_start[:, None]))
    overlap_t = jnp.asarray(overlap_t.astype(np.float32), dtype=BF16)
    wq = heads_per_group * HEAD_DIM
    return pl.pallas_call(
        functools.partial(_cmp_select_kernel, tq=tq, heads_per_group=heads_per_group, n_blocks_pad=n_blocks_pad),
        grid=(n_groups, s // tq),
        in_specs=[pl.BlockSpec((tq, wq), lambda g, i: (i, g)),
                  pl.BlockSpec((1, 1, n_cmp, HEAD_DIM), lambda g, i: (0, g, 0, 0)),
                  pl.BlockSpec((1, 1, n_cmp, HEAD_DIM), lambda g, i: (1, g, 0, 0)),
                  pl.BlockSpec((n_blocks_pad, n_cmp), lambda g, i: (0, 0)),
                  pl.BlockSpec((1, tq, LANES), lambda g, i: (g, i, 0))],
        out_specs=[pl.BlockSpec((tq, wq), lambda g, i: (i, g)),
                   pl.BlockSpec((1, tq, n_blocks_pad), lambda g, i: (g, i, 0))],
        out_shape=[jax.ShapeDtypeStruct(qn.shape, F32),
                   jax.ShapeDtypeStruct((n_groups, s, n_blocks_pad), BF16)],
        compiler_params=_params(("arbitrary", "arbitrary")),
    )(qn, kcv, kcv, overlap_t, gates)


def _slc_attention(qn, sel, ksk, vs, prev, gates, heads_per_group):
    n_groups, s, _ = ksk.shape
    tq, tk = 256, 256
    wq = heads_per_group * HEAD_DIM
    in_specs = [
        pl.BlockSpec((tq, wq), lambda g, i: (i, g)),
        pl.BlockSpec((1, tq, LANES), lambda g, i: (g, i, 0)),
        pl.BlockSpec((1, s, 2 * LANES), lambda g, i: (g, 0, 0)),
        pl.BlockSpec((1, s, 2 * LANES), lambda g, i: (g, 0, 0)),
        pl.BlockSpec((tq, wq), lambda g, i: (i, g)),
        pl.BlockSpec((1, tq, LANES), lambda g, i: (g, i, 0)),
    ]
    out_spec = pl.BlockSpec((tq, wq), lambda g, i: (i, g))
    return _attn_call("slc", n_groups, s // tq, in_specs, out_spec, jax.ShapeDtypeStruct(qn.shape, F32),
                      2 * LANES, (qn, sel, ksk, vs, prev, gates), tq=tq, tk=tk, nrep=heads_per_group,
                      group_sizes=(4, 1))


def _win_attention(qn, kw, vw, prev, gates, heads_per_group):
    n_groups, s, _ = kw.shape
    tq = tk = 256
    assert 2 * tk == WINDOW
    wq = heads_per_group * HEAD_DIM
    in_specs = [
        pl.BlockSpec((tq, wq), lambda g, i: (i, g)),
        pl.BlockSpec((1, s, LANES), lambda g, i: (g, 0, 0)),
        pl.BlockSpec((1, s, 2 * LANES), lambda g, i: (g, 0, 0)),
        pl.BlockSpec((tq, wq), lambda g, i: (i, g)),
        pl.BlockSpec((1, tq, LANES), lambda g, i: (g, i, 0)),
    ]
    out_spec = pl.BlockSpec((tq, wq), lambda g, i: (i, g))
    return _attn_call("win", n_groups, s // tq, in_specs, out_spec, jax.ShapeDtypeStruct(qn.shape, BF16),
                      LANES, (qn, kw, vw, prev, gates), tq=tq, tk=tk, nrep=heads_per_group)


def _rope_tables(positions, dim):
    inv = ROPE_THETA ** (-jnp.arange(0, dim, 2, dtype=F32) / dim)
    ang = positions.astype(F32)[:, None] * inv
    cos, sin = jnp.cos(ang), jnp.sin(ang)
    reps = LANES // dim
    cos_t = jnp.tile(jnp.concatenate([cos, cos], axis=-1), (1, reps))
    sin_t = jnp.tile(jnp.concatenate([-sin, sin], axis=-1), (1, reps))
    return cos_t, sin_t


def _pad_cols(w, n):
    return jnp.pad(w, ((0, 0), (0, n - w.shape[1])))


def _even_mixer(h, w_in, b_forget, gain_diff, gain_fox, lam_params, subln, tables, lam_init):
    d_model = h.shape[1]
    n_heads = d_model // (2 * HEAD_DIM)
    n_in = 6 * n_heads * HEAD_DIM + 1024
    proj = _matmul(h, _pad_cols(w_in, n_in).astype(BF16), bm=1024, bn=1024, bk=d_model)
    qa, ka, va, qf, kf, vf = _even_prep(proj, *tables, gain_diff, gain_fox, b_forget, n_heads)
    return [_diff_attention(qa, ka, va, lam_params, subln, lam_init), _fox_attention(qf, kf, vf)]


def _odd_mixer(h, w_in, q_gain, k_gain, cmp_pos, cmp_w1, cmp_w2, tables):
    s, d_model = h.shape
    n_heads, n_groups = d_model // HEAD_DIM, N_KV_GROUPS
    n_in = d_model + 7 * n_groups * HEAD_DIM
    proj = _matmul(h, _pad_cols(w_in, n_in).astype(BF16), bm=1024, bn=n_in // 10, bk=d_model)
    (qn, ktop, kbot, vtop, vbot, ksk, vs, kw, vw, gates) = _odd_prep(
        proj, *tables, q_gain, k_gain, cmp_pos, n_heads, n_groups)

    def halves(k_part, v_part):
        a = jnp.stack([k_part, v_part]).reshape(2, s // CMP_STRIDE, CMP_STRIDE, n_groups, HEAD_DIM)
        return jnp.transpose(a, (0, 3, 1, 2, 4)).reshape(2, n_groups, s // CMP_STRIDE, CMP_STRIDE * HEAD_DIM)

    kcv = _compress(halves(ktop, vtop), halves(kbot, vbot), cmp_w1, cmp_w2, k_gain[0])
    hpg = n_heads // n_groups
    o_cmp, sel = _cmp_select(qn, kcv, gates, n_groups, hpg)
    o_cmp_slc = _slc_attention(qn, sel, ksk, vs, o_cmp, gates, hpg)
    return [_win_attention(qn, kw, vw, o_cmp_slc, gates, hpg)]


def kernel(x, c, positions, w_ada, b_ada, norm_gain, even_w_in, even_b_forget, even_qk_gain_diff,
           even_qk_gain_fox, even_diff_lambda, even_diff_subln, even_w_out, odd_w_in, odd_q_gain,
           odd_k_gain, odd_cmp_pos, odd_cmp_w1, odd_cmp_w2, odd_w_out, mlp_w1, mlp_w2):
    batch, s, d_model = x.shape
    assert batch == 1
    depth = w_ada.shape[0]
    xs = x[0]
    pos = positions[0]
    tables64 = _rope_tables(pos, DIFF_QK_DIM)
    tables128 = _rope_tables(pos, HEAD_DIM)
    mod = _adaln(c, w_ada, b_ada)
    for i in range(depth):
        sh1, sc1, g1, sh2, sc2, g2 = [mod[i, :, k * d_model:(k + 1) * d_model] for k in range(6)]
        h = _norm_mod(xs, norm_gain[i, 0], sc1, sh1)
        if i % 2 == 0:
            e = i // 2
            lam_init = 0.8 - 0.6 * math.exp(-0.3 * i)
            o_parts = _even_mixer(h, even_w_in[e], even_b_forget[e], even_qk_gain_diff[e], even_qk_gain_fox[e],
                                  even_diff_lambda[e], even_diff_subln[e], tables64, lam_init)
            w_out, w_layer = even_w_out, e
        else:
            od = i // 2
            o_parts = _odd_mixer(h, odd_w_in[od], odd_q_gain[od], odd_k_gain[od], odd_cmp_pos[od],
                                 odd_cmp_w1[od], odd_cmp_w2[od], tables128)
            w_out, w_layer = odd_w_out, od
        xs = _matmul_resident(o_parts, w_out, w_layer, bm=1024, bn=512, epilogue="resid", out_dtype=F32,
                              resid=xs, gate=g1)
        h = _norm_mod(xs, norm_gain[i, 1], sc2, sh2)
        ff = _matmul_resident([h], mlp_w1, i, bm=1024, bn=512, epilogue="relu2", out_dtype=BF16)
        xs = _matmul(ff, mlp_w2[i].astype(BF16), bm=1024, bn=1024, bk=2048, epilogue="resid", resid=xs, gate=g2)
    return xs[None]
```

```python
import functools
import math

import numpy as np
import jax
import jax.numpy as jnp
from jax import lax
from jax.experimental import pallas as pl
from jax.experimental.pallas import tpu as pltpu

F32 = jnp.float32
BF16 = jnp.bfloat16

HEAD_DIM = 128
DIFF_QK_DIM = 64
N_KV_GROUPS = 4
CMP_BLOCK = 32
CMP_STRIDE = 16
SLC_BLOCK = 64
SLC_TOPK = 16
WINDOW = 512
ROPE_THETA = 10000.0
EPS = 1e-6
NEG = -1e30
BIG = 1e30
TINY = 1e-20
UNSELECTED_SCORE = 2.0 ** 99
LOG2E = 1.4426950408889634
LANES = 128
VMEM_LIMIT = 56 * 1024 * 1024

NT_DIMS = (((1,), (1,)), ((), ()))


def _params(sem):
    return pltpu.CompilerParams(dimension_semantics=sem, vmem_limit_bytes=VMEM_LIMIT)


def _adaln_kernel(cb_ref, w_ref, b_ref, o_ref):
    d_model, tn = w_ref.shape[1], w_ref.shape[2]
    rows_per_step = 256

    def body(r, acc):
        rows = pl.ds(pl.multiple_of(r * rows_per_step, rows_per_step), rows_per_step)
        cb = cb_ref[rows, :]
        cond = cb * jax.nn.sigmoid(cb)
        prod = w_ref[0, rows, :] * jnp.concatenate([cond] * (tn // LANES), axis=1)
        return acc + prod.reshape(rows_per_step // 8, 8, tn).sum(axis=0)

    acc = lax.fori_loop(0, d_model // rows_per_step, body, jnp.zeros((8, tn), F32))
    o_ref[0] = acc.sum(axis=0, keepdims=True) + b_ref[0]


def _adaln(c, w_ada, b_ada):
    depth, d_model, n_out = w_ada.shape
    tn = 512
    cb = jnp.broadcast_to(c.reshape(d_model, 1), (d_model, LANES))
    return pl.pallas_call(
        _adaln_kernel,
        grid=(depth, n_out // tn),
        in_specs=[
            pl.BlockSpec((d_model, LANES), lambda l, n: (0, 0)),
            pl.BlockSpec((1, d_model, tn), lambda l, n: (l, 0, n)),
            pl.BlockSpec((1, 1, tn), lambda l, n: (l, 0, n)),
        ],
        out_specs=pl.BlockSpec((1, 1, tn), lambda l, n: (l, 0, n)),
        out_shape=jax.ShapeDtypeStruct((depth, 1, n_out), F32),
        compiler_params=_params(("arbitrary", "arbitrary")),
    )(cb, w_ada, b_ada.reshape(depth, 1, n_out))


def _norm_mod_kernel(x_ref, g_ref, sc_ref, sh_ref, o_ref):
    x = x_ref[...]
    ms = jnp.mean(x * x, axis=-1, keepdims=True)
    y = x * lax.rsqrt(ms + EPS) * g_ref[...]
    o_ref[...] = (y * (1.0 + sc_ref[...]) + sh_ref[...]).astype(o_ref.dtype)


def _norm_mod(x, gain, scale, shift):
    s, d_model = x.shape
    ts = 256
    row = pl.BlockSpec((1, d_model), lambda i: (0, 0))
    return pl.pallas_call(
        _norm_mod_kernel,
        grid=(s // ts,),
        in_specs=[pl.BlockSpec((ts, d_model), lambda i: (i, 0)), row, row, row],
        out_specs=pl.BlockSpec((ts, d_model), lambda i: (i, 0)),
        out_shape=jax.ShapeDtypeStruct((s, d_model), BF16),
        compiler_params=_params(("arbitrary",)),
    )(x, gain.reshape(1, d_model), scale, shift)


def _mm_kernel(*refs, nk, epilogue):
    if epilogue == "resid":
        a_ref, b_ref, x_ref, g_ref, o_ref = refs[:5]
    else:
        a_ref, b_ref, o_ref = refs[:3]

    def finish(acc):
        if epilogue == "plain":
            o_ref[...] = acc.astype(o_ref.dtype)
        elif epilogue == "relu2":
            r = jnp.maximum(acc, 0.0)
            o_ref[...] = (r * r).astype(o_ref.dtype)
        else:
            o_ref[...] = x_ref[...] + g_ref[...] * acc

    if nk == 1:
        finish(jnp.dot(a_ref[...], b_ref[...], preferred_element_type=F32))
    else:
        acc_ref = refs[-1]
        k = pl.program_id(2)

        @pl.when(k == 0)
        def _():
            acc_ref[...] = jnp.zeros(acc_ref.shape, F32)

        acc_ref[...] += jnp.dot(a_ref[...], b_ref[...], preferred_element_type=F32)

        @pl.when(k == nk - 1)
        def _():
            finish(acc_ref[...])


def _matmul(a, b, *, bm, bn, bk, epilogue="plain", out_dtype=F32, resid=None, gate=None):
    m, kdim = a.shape
    n = b.shape[1]
    nk = kdim // bk
    in_specs = [pl.BlockSpec((bm, bk), lambda i, j, k: (i, k)),
                pl.BlockSpec((bk, bn), lambda i, j, k: (k, j))]
    args = [a, b]
    if epilogue == "resid":
        in_specs += [pl.BlockSpec((bm, bn), lambda i, j, k: (i, j)),
                     pl.BlockSpec((1, bn), lambda i, j, k: (0, j))]
        args += [resid, gate]
    scratch = [pltpu.VMEM((bm, bn), F32)] if nk > 1 else []
    return pl.pallas_call(
        functools.partial(_mm_kernel, nk=nk, epilogue=epilogue),
        grid=(m // bm, n // bn, nk),
        in_specs=in_specs,
        out_specs=pl.BlockSpec((bm, bn), lambda i, j, k: (i, j)),
        out_shape=jax.ShapeDtypeStruct((m, n), out_dtype),
        scratch_shapes=scratch,
        compiler_params=_params(("arbitrary", "arbitrary", "arbitrary")),
    )(*args)


def _mm_resident_kernel(*refs, n_a, epilogue):
    a_refs, b_ref = refs[:n_a], refs[n_a]
    if epilogue == "resid":
        x_ref, g_ref, o_ref, b_bf_ref = refs[n_a + 1:]
    else:
        o_ref, b_bf_ref = refs[n_a + 1:]

    @pl.when(pl.program_id(1) == 0)
    def _():
        b_bf_ref[...] = b_ref[0].astype(BF16)

    k_part = b_bf_ref.shape[0] // n_a
    acc = jnp.dot(a_refs[0][...], b_bf_ref[0:k_part, :], preferred_element_type=F32)
    for t in range(1, n_a):
        acc = acc + jnp.dot(a_refs[t][...], b_bf_ref[t * k_part:(t + 1) * k_part, :], preferred_element_type=F32)
    if epilogue == "relu2":
        r = jnp.maximum(acc, 0.0)
        o_ref[...] = (r * r).astype(o_ref.dtype)
    else:
        o_ref[...] = x_ref[...] + g_ref[...] * acc


def _matmul_resident(a_parts, w, layer, *, bm, bn, epilogue, out_dtype, resid=None, gate=None):
    m = a_parts[0].shape[0]
    _, kdim, n = w.shape
    n_a = len(a_parts)
    in_specs = [pl.BlockSpec((bm, kdim // n_a), lambda j, i: (i, 0)) for _ in a_parts]
    in_specs.append(pl.BlockSpec((1, kdim, bn), lambda j, i: (layer, 0, j)))
    args = list(a_parts) + [w]
    if epilogue == "resid":
        in_specs += [pl.BlockSpec((bm, bn), lambda j, i: (i, j)), pl.BlockSpec((1, bn), lambda j, i: (0, j))]
        args += [resid, gate]
    return pl.pallas_call(
        functools.partial(_mm_resident_kernel, n_a=n_a, epilogue=epilogue),
        grid=(n // bn, m // bm),
        in_specs=in_specs,
        out_specs=pl.BlockSpec((bm, bn), lambda j, i: (i, j)),
        out_shape=jax.ShapeDtypeStruct((m, n), out_dtype),
        scratch_shapes=[pltpu.VMEM((kdim, bn), BF16)],
        compiler_params=_params(("arbitrary", "arbitrary")),
    )(*args)


def _rms(x, n_chunk, gain):
    sq = x * x
    if n_chunk == LANES:
        ss = jnp.sum(sq, axis=-1, keepdims=True)
    else:
        lane = lax.broadcasted_iota(jnp.int32, x.shape, 1)
        low = lane < n_chunk
        s_low = jnp.sum(jnp.where(low, sq, 0.0), axis=-1, keepdims=True)
        s_all = jnp.sum(sq, axis=-1, keepdims=True)
        ss = jnp.where(low, s_low, s_all - s_low)
    return x * lax.rsqrt(ss * (1.0 / n_chunk) + EPS) * gain


def _rope(x, cos_t, sin_t, n_chunk):
    half = n_chunk // 2
    if n_chunk == LANES:
        partner = pltpu.roll(x, half, 1)
    else:
        lane = lax.broadcasted_iota(jnp.int32, x.shape, 1)
        first = (lane % n_chunk) < half
        partner = jnp.where(first, pltpu.roll(x, LANES - half, 1), pltpu.roll(x, half, 1))
    return x * cos_t + partner * sin_t


def _split3(v):
    hi = v.astype(BF16)
    r1 = v - hi.astype(F32)
    mid = r1.astype(BF16)
    lo = (r1 - mid.astype(F32)).astype(BF16)
    return hi, mid, lo


def _even_prep_kernel(qa_ref, ka_ref, va_ref, qb_ref, kb_ref, vb_ref, fb_ref, cos_ref, sin_ref,
                      gdq_ref, gdk_ref, gfq_ref, gfk_ref, bf_ref,
                      qa_o, ka_o, va_o, qf_o, kf_o, vf_o, carry_ref, *, n_heads):
    i = pl.program_id(0)
    ts = qa_ref.shape[0]
    cos_t, sin_t = cos_ref[...], sin_ref[...]
    lane = lax.broadcasted_iota(jnp.int32, (ts, LANES), 1)
    ones_aug = jnp.where(lane < 3, 1.0, 0.0).astype(BF16)
    ones = jnp.ones((ts, LANES), BF16)

    z = fb_ref[...] + bf_ref[...]
    logf = jnp.minimum(z, 0.0) - jnp.log(1.0 + jnp.exp(-jnp.abs(z)))
    r = lax.broadcasted_iota(jnp.int32, (ts, ts), 0)
    c = lax.broadcasted_iota(jnp.int32, (ts, ts), 1)
    tri = jnp.where(c <= r, 1.0, 0.0).astype(BF16)

    @pl.when(i == 0)
    def _():
        carry_ref[...] = jnp.zeros_like(carry_ref)

    f_cum = carry_ref[0:1, :]
    for part in _split3(logf):
        f_cum = f_cum + jnp.dot(tri, part, preferred_element_type=F32)
    carry_ref[...] = jnp.broadcast_to(f_cum[ts - 1:ts, :], carry_ref.shape)
    neg_hi, neg_mid, neg_lo = _split3(f_cum * (-LOG2E))
    prow = lax.broadcasted_iota(jnp.int32, (LANES, LANES), 0)
    pcol = lax.broadcasted_iota(jnp.int32, (LANES, LANES), 1)

    for h in range(n_heads):
        sl = slice(h * LANES, (h + 1) * LANES)
        qa = _rope(_rms(qa_ref[:, sl], DIFF_QK_DIM, gdq_ref[...]), cos_t, sin_t, DIFF_QK_DIM)
        qa_o[h] = (qa * (DIFF_QK_DIM ** -0.5 * LOG2E)).astype(BF16)
        ka = _rope(_rms(ka_ref[:, sl], DIFF_QK_DIM, gdk_ref[...]), cos_t, sin_t, DIFF_QK_DIM)
        ka_o[h] = ka.astype(BF16)
        va_o[h, :, 0:LANES] = va_ref[:, sl].astype(BF16)
        va_o[h, :, LANES:2 * LANES] = ones
        qb = _rms(qb_ref[:, sl], HEAD_DIM, gfq_ref[...]) * (HEAD_DIM ** -0.5 * LOG2E)
        qf_o[h, :, 0:LANES] = qb.astype(BF16)
        qf_o[h, :, LANES:2 * LANES] = ones_aug
        kb = _rms(kb_ref[:, sl], HEAD_DIM, gfk_ref[...])
        kf_o[h, :, 0:LANES] = kb.astype(BF16)
        aug = jnp.zeros((ts, LANES), F32)
        for lane_idx, part in enumerate((neg_hi, neg_mid, neg_lo)):
            place = jnp.where((prow == h) & (pcol == lane_idx), 1.0, 0.0).astype(BF16)
            aug = aug + jnp.dot(part, place, preferred_element_type=F32)
        kf_o[h, :, LANES:2 * LANES] = aug.astype(BF16)
        vf_o[h, :, 0:LANES] = vb_ref[:, sl].astype(BF16)
        vf_o[h, :, LANES:2 * LANES] = ones


def _even_prep(proj, cos_t, sin_t, gain_diff, gain_fox, b_forget, n_heads):
    s = proj.shape[0]
    ts = 256
    w = n_heads * HEAD_DIM
    seg = lambda k: pl.BlockSpec((ts, w), lambda i, k=k: (i, k))
    tab = pl.BlockSpec((ts, LANES), lambda i: (i, 0))
    row = pl.BlockSpec((1, LANES), lambda i: (0, 0))
    gdq = jnp.tile(gain_diff[0], 2).reshape(1, LANES)
    gdk = jnp.tile(gain_diff[1], 2).reshape(1, LANES)
    bfp = jnp.zeros((1, LANES), F32).at[0, :n_heads].set(b_forget)
    narrow = jax.ShapeDtypeStruct((n_heads, s, LANES), BF16)
    wide = jax.ShapeDtypeStruct((n_heads, s, 2 * LANES), BF16)
    narrow_spec = pl.BlockSpec((n_heads, ts, LANES), lambda i: (0, i, 0))
    wide_spec = pl.BlockSpec((n_heads, ts, 2 * LANES), lambda i: (0, i, 0))
    return pl.pallas_call(
        functools.partial(_even_prep_kernel, n_heads=n_heads),
        grid=(s // ts,),
        in_specs=[seg(0), seg(1), seg(2), seg(3), seg(4), seg(5),
                  pl.BlockSpec((ts, LANES), lambda i: (i, 6 * w // LANES)),
                  tab, tab, row, row, row, row, row],
        out_specs=[narrow_spec, narrow_spec, wide_spec, wide_spec, wide_spec, wide_spec],
        out_shape=[narrow, narrow, wide, wide, wide, wide],
        scratch_shapes=[pltpu.VMEM((8, LANES), F32)],
        compiler_params=_params(("arbitrary",)),
    )(proj, proj, proj, proj, proj, proj, proj, cos_t, sin_t, gdq, gdk,
      gain_fox[0].reshape(1, LANES), gain_fox[1].reshape(1, LANES), bfp)


MASK_DIAG, MASK_EDGE = 0, 1


def _attn_kernel(*refs, mode, tq, tk, nrep, group_sizes, lam_init):
    bias_ref, refs = refs[0], refs[1:]
    if mode == "diff":
        q_ref, k_ref, v_ref, lam_ref, sub_ref, o_ref, qs_ref, m_ref, acc_ref = refs
    elif mode == "fox":
        q_ref, k_ref, v_ref, o_ref, m_ref, acc_ref = refs
    elif mode == "slc":
        q_ref, sel_ref, k_ref, v_ref, prev_ref, gate_ref, o_ref, qs_ref, m_ref, acc_ref = refs
    else:
        q_ref, k_ref, v_ref, prev_ref, gate_ref, o_ref, qs_ref, m_ref, acc_ref = refs
    i = pl.program_id(1)
    rows = nrep * tq
    q_lo = i * tq
    blk = min(tq, tk)
    n_blk = tq // blk

    def row0(b, rep):
        return (b * nrep + rep) * blk

    for b in range(n_blk):
        qrows = slice(b * blk, (b + 1) * blk)
        if mode == "diff":
            q = q_ref[0, qrows, :]
            lane = lax.broadcasted_iota(jnp.int32, q.shape, 1)
            qs_ref[row0(b, 0):row0(b, 0) + blk, :] = jnp.where(lane < DIFF_QK_DIM, q, jnp.zeros_like(q))
            qs_ref[row0(b, 1):row0(b, 1) + blk, :] = jnp.where(lane >= DIFF_QK_DIM, q, jnp.zeros_like(q))
        elif mode in ("slc", "win"):
            for hh in range(nrep):
                dst = slice(row0(b, hh), row0(b, hh) + blk)
                qs_ref[dst, 0:LANES] = q_ref[qrows, hh * LANES:(hh + 1) * LANES]
                if mode == "slc":
                    qs_ref[dst, LANES:2 * LANES] = sel_ref[0, qrows, :]
    m_ref[...] = jnp.full(m_ref.shape, NEG, F32)
    acc_ref[...] = jnp.zeros(acc_ref.shape, F32)
    n_streams = m_ref.shape[0]

    def process(chunks):
        for c, masked, st, r0 in chunks:
            nrows = rows - r0
            k0 = pl.multiple_of(c * tk, tk)
            k = k_ref[0, pl.ds(k0, tk), :]
            v = v_ref[0, pl.ds(k0, tk), :]
            q_all = q_ref[0, r0:rows, :] if mode == "fox" else qs_ref[r0:rows, :]
            s = lax.dot_general(q_all, k, NT_DIMS, preferred_element_type=F32)
            if masked is not None:
                nb = nrep * blk
                top = s[0:nb] + bias_ref[masked]
                s = top if nb == nrows else jnp.concatenate([top, s[nb:nrows]], axis=0)
            tiles = [s[:, t * LANES:(t + 1) * LANES] for t in range(tk // LANES)]
            tile_max = tiles[0]
            for t in tiles[1:]:
                tile_max = jnp.maximum(tile_max, t)
            m_prev = m_ref[st, r0:rows, :]
            m_new = jnp.maximum(m_prev, jnp.max(tile_max, axis=-1, keepdims=True))
            alpha = jnp.exp2(m_prev - m_new)
            p = jnp.concatenate([jnp.exp2(t - m_new) for t in tiles], axis=1).astype(BF16)
            pv = jnp.dot(p, v, preferred_element_type=F32)
            acc_ref[st, r0:rows, 0:LANES] = alpha * acc_ref[st, r0:rows, 0:LANES] + pv[:, 0:LANES]
            acc_ref[st, r0:rows, LANES:2 * LANES] = (alpha * acc_ref[st, r0:rows, LANES:2 * LANES]
                                                     + pv[:, LANES:2 * LANES])
            m_ref[st, r0:rows, :] = m_new

    if mode == "win":
        @pl.when(i >= 2)
        def _():
            process([(i - 2, MASK_EDGE, 0, 0), (i - 1, None, 1, 0), (i, MASK_DIAG, 2, 0)])

        @pl.when(i == 1)
        def _():
            process([(i - 1, None, 1, 0), (i, MASK_DIAG, 2, 0)])

        @pl.when(i == 0)
        def _():
            process([(i, MASK_DIAG, 2, 0)])
    else:
        n_full = q_lo // tk
        start = 0
        for n in group_sizes:
            count = (n_full - start) // n

            def group_body(g, carry, n=n, start=start):
                process([(start + g * n + j, None, j, 0) for j in range(n)])
                return carry

            lax.fori_loop(0, count, group_body, 0)
            start = start + count * n
        process([(n_full + j, MASK_DIAG, j, row0(j, 0)) for j in range(n_blk)])

    m_all = m_ref[0]
    for st in range(1, n_streams):
        m_all = jnp.maximum(m_all, m_ref[st])
    num = jnp.zeros((rows, LANES), F32)
    den = jnp.zeros((rows, LANES), F32)
    for st in range(n_streams):
        w = jnp.exp2(m_ref[st] - m_all)
        num = num + w * acc_ref[st, :, 0:LANES]
        den = den + w * acc_ref[st, :, LANES:2 * LANES]
    o = num / den
    if mode == "diff":
        lp = lam_ref[...]
        lam = (jnp.exp(jnp.sum(lp[0:1] * lp[1:2], axis=-1, keepdims=True))
               - jnp.exp(jnp.sum(lp[2:3] * lp[3:4], axis=-1, keepdims=True)) + lam_init)
    for b in range(n_blk):
        qrows = slice(b * blk, (b + 1) * blk)
        if mode == "diff":
            od = o[row0(b, 0):row0(b, 0) + blk] - lam * o[row0(b, 1):row0(b, 1) + blk]
            ms = jnp.mean(od * od, axis=-1, keepdims=True)
            o_ref[qrows, :] = (od * lax.rsqrt(ms + EPS) * sub_ref[...] * (1.0 - lam_init)).astype(o_ref.dtype)
        elif mode == "fox":
            o_ref[qrows, :] = o[qrows].astype(o_ref.dtype)
        else:
            branch = 1 if mode == "slc" else 2
            gates = gate_ref[0, qrows, :]
            for hh in range(nrep):
                cols = slice(hh * LANES, (hh + 1) * LANES)
                gate = gates[:, 3 * hh + branch:3 * hh + branch + 1]
                o_ref[qrows, cols] = (prev_ref[qrows, cols]
                                      + gate * o[row0(b, hh):row0(b, hh) + blk]).astype(o_ref.dtype)


def _mask_bias(nrep, tk, with_edge):
    row = np.tile(np.arange(tk), nrep)[:, None]
    col = np.arange(tk)[None, :]
    diag = np.where(col <= row, 0.0, NEG).astype(np.float32)
    kinds = [diag, np.where(col <= row, NEG, 0.0).astype(np.float32)] if with_edge else [diag]
    return jnp.asarray(np.stack(kinds))


def _attn_call(mode, n_outer, n_q, in_specs, out_spec, out_shape, qs_width, args, *, tq, tk, nrep,
               group_sizes=(), lam_init=0.0):
    assert tq % tk == 0
    rows = nrep * tq
    n_streams = max(max(group_sizes, default=1), tq // tk, 3 if mode == "win" else 1)
    bias = _mask_bias(nrep, tk, with_edge=(mode == "win"))
    scratch = [] if qs_width is None else [pltpu.VMEM((rows, qs_width), BF16)]
    scratch += [pltpu.VMEM((n_streams, rows, LANES), F32), pltpu.VMEM((n_streams, rows, 2 * LANES), F32)]
    bias_spec = pl.BlockSpec(bias.shape, lambda h, i: (0, 0, 0))
    return pl.pallas_call(
        functools.partial(_attn_kernel, mode=mode, tq=tq, tk=tk, nrep=nrep, group_sizes=group_sizes,
                          lam_init=lam_init),
        grid=(n_outer, n_q), in_specs=[bias_spec] + in_specs, out_specs=out_spec, out_shape=out_shape,
        scratch_shapes=scratch,
        compiler_params=_params(("arbitrary", "arbitrary")),
    )(bias, *args)


def _diff_attention(qa, ka, va, lam_params, subln, lam_init):
    n_heads, s, _ = qa.shape
    tq, tk = 1024, 256
    in_specs = [
        pl.BlockSpec((1, tq, LANES), lambda h, i: (h, i, 0)),
        pl.BlockSpec((1, s, LANES), lambda h, i: (h, 0, 0)),
        pl.BlockSpec((1, s, 2 * LANES), lambda h, i: (h, 0, 0)),
        pl.BlockSpec((4, DIFF_QK_DIM), lambda h, i: (0, 0)),
        pl.BlockSpec((1, LANES), lambda h, i: (0, 0)),
    ]
    out_spec = pl.BlockSpec((tq, LANES), lambda h, i: (i, h))
    return _attn_call("diff", n_heads, s // tq, in_specs, out_spec,
                      jax.ShapeDtypeStruct((s, n_heads * HEAD_DIM), BF16), LANES,
                      (qa, ka, va, lam_params, subln.reshape(1, LANES)),
                      tq=tq, tk=tk, nrep=2, group_sizes=(4,), lam_init=lam_init)


def _fox_attention(qf, kf, vf):
    n_heads, s, _ = qf.shape
    tq, tk = 1024, 256
    in_specs = [
        pl.BlockSpec((1, tq, 2 * LANES), lambda h, i: (h, i, 0)),
        pl.BlockSpec((1, s, 2 * LANES), lambda h, i: (h, 0, 0)),
        pl.BlockSpec((1, s, 2 * LANES), lambda h, i: (h, 0, 0)),
    ]
    out_spec = pl.BlockSpec((tq, LANES), lambda h, i: (i, h))
    return _attn_call("fox", n_heads, s // tq, in_specs, out_spec,
                      jax.ShapeDtypeStruct((s, n_heads * HEAD_DIM), BF16), None,
                      (qf, kf, vf), tq=tq, tk=tk, nrep=1, group_sizes=(8, 4))


def _odd_prep_kernel(q_ref, kc_ref, vc_ref, ks_ref, vs_ref, kw_ref, vw_ref, gl_ref, cos_ref, sin_ref,
                     qg_ref, kg_ref, pe_ref,
                     qn_o, ktop_o, kbot_o, vtop_o, vbot_o, ksk_o, vs_o, kw_o, vw_o, gates_o,
                     *, n_heads, n_groups):
    i = pl.program_id(0)
    ts = q_ref.shape[0]
    cos_t, sin_t = cos_ref[...], sin_ref[...]
    ones = jnp.ones((ts, LANES), BF16)
    for h in range(n_heads):
        sl = slice(h * LANES, (h + 1) * LANES)
        q = _rope(_rms(q_ref[:, sl], HEAD_DIM, qg_ref[...]), cos_t, sin_t, HEAD_DIM)
        qn_o[:, sl] = (q * (HEAD_DIM ** -0.5 * LOG2E)).astype(BF16)
    key_block = (i * ts + lax.broadcasted_iota(jnp.int32, (ts, LANES), 0)) // SLC_BLOCK
    lane = lax.broadcasted_iota(jnp.int32, (ts, LANES), 1)
    block_onehot = jnp.where(lane == key_block, UNSELECTED_SCORE, 0.0).astype(BF16)
    for g in range(n_groups):
        sl = slice(g * LANES, (g + 1) * LANES)
        kc = _rope(kc_ref[:, sl], cos_t, sin_t, HEAD_DIM)
        ktop_o[:, sl] = (kc + pe_ref[0]).astype(BF16)
        kbot_o[:, sl] = (kc + pe_ref[1]).astype(BF16)
        vc = vc_ref[:, sl]
        vtop_o[:, sl] = (vc + pe_ref[2]).astype(BF16)
        vbot_o[:, sl] = (vc + pe_ref[3]).astype(BF16)
        ks = _rope(_rms(ks_ref[:, sl], HEAD_DIM, kg_ref[1:2, :]), cos_t, sin_t, HEAD_DIM)
        ksk_o[g, :, 0:LANES] = ks.astype(BF16)
        ksk_o[g, :, LANES:2 * LANES] = block_onehot
        kw = _rope(_rms(kw_ref[:, sl], HEAD_DIM, kg_ref[2:3, :]), cos_t, sin_t, HEAD_DIM)
        kw_o[g] = kw.astype(BF16)
        vs_o[g, :, 0:LANES] = vs_ref[:, sl].astype(BF16)
        vs_o[g, :, LANES:2 * LANES] = ones
        vw_o[g, :, 0:LANES] = vw_ref[:, sl].astype(BF16)
        vw_o[g, :, LANES:2 * LANES] = ones
    gates = jax.nn.sigmoid(gl_ref[...])
    lanes_per_group = 3 * n_heads // n_groups
    for g in range(n_groups):
        gates_o[g] = gates if g == 0 else pltpu.roll(gates, LANES - g * lanes_per_group, 1)


def _odd_prep(proj, cos_t, sin_t, q_gain, k_gain, cmp_pos, n_heads, n_groups):
    s = proj.shape[0]
    ts = 256
    wq, wg = n_heads * HEAD_DIM, n_groups * HEAD_DIM
    seg = lambda k: pl.BlockSpec((ts, wg), lambda i, k=k: (i, wq // wg + k))
    tab = pl.BlockSpec((ts, LANES), lambda i: (i, 0))
    grp = pl.BlockSpec((ts, wg), lambda i: (i, 0))
    narrow_spec = pl.BlockSpec((n_groups, ts, LANES), lambda i: (0, i, 0))
    wide_spec = pl.BlockSpec((n_groups, ts, 2 * LANES), lambda i: (0, i, 0))
    pe_tiles = jnp.stack([jnp.tile(cmp_pos[kv, half * CMP_STRIDE:(half + 1) * CMP_STRIDE], (ts // CMP_STRIDE, 1))
                          for kv in range(2) for half in range(2)])
    g_bf = jax.ShapeDtypeStruct((s, wg), BF16)
    narrow = jax.ShapeDtypeStruct((n_groups, s, LANES), BF16)
    wide = jax.ShapeDtypeStruct((n_groups, s, 2 * LANES), BF16)
    return pl.pallas_call(
        functools.partial(_odd_prep_kernel, n_heads=n_heads, n_groups=n_groups),
        grid=(s // ts,),
        in_specs=[pl.BlockSpec((ts, wq), lambda i: (i, 0)), seg(0), seg(1), seg(2), seg(3), seg(4), seg(5),
                  pl.BlockSpec((ts, LANES), lambda i: (i, (wq + 6 * wg) // LANES)),
                  tab, tab,
                  pl.BlockSpec((1, LANES), lambda i: (0, 0)),
                  pl.BlockSpec((3, LANES), lambda i: (0, 0)),
                  pl.BlockSpec((4, ts, LANES), lambda i: (0, 0, 0))],
        out_specs=[pl.BlockSpec((ts, wq), lambda i: (i, 0)), grp, grp, grp, grp,
                   wide_spec, wide_spec, narrow_spec, wide_spec, narrow_spec],
        out_shape=[jax.ShapeDtypeStruct((s, wq), BF16), g_bf, g_bf, g_bf, g_bf,
                   wide, wide, narrow, wide, jax.ShapeDtypeStruct((n_groups, s, LANES), F32)],
        compiler_params=_params(("arbitrary",)),
    )(proj, proj, proj, proj, proj, proj, proj, proj, cos_t, sin_t,
      q_gain.reshape(1, LANES), k_gain, pe_tiles)


def _compress_kernel(top_ref, bot_ref, w1_ref, w2_ref, kg_ref, o_ref):
    kv = pl.program_id(0)
    n_half = top_ref.shape[2]
    k_half = top_ref.shape[3]
    h_top = jnp.dot(top_ref[0, 0], w1_ref[0, 0:k_half, :].astype(BF16), preferred_element_type=F32)
    h_bot = jnp.dot(bot_ref[0, 0], w1_ref[0, k_half:2 * k_half, :].astype(BF16), preferred_element_type=F32)
    hid = h_top + pltpu.roll(h_bot, n_half - 1, 0)
    act = hid * jax.nn.sigmoid(hid)
    out = jnp.dot(act.astype(BF16), w2_ref[0].astype(BF16), preferred_element_type=F32)
    ms = jnp.mean(out * out, axis=-1, keepdims=True)
    normed = out * lax.rsqrt(ms + EPS) * kg_ref[...]
    o_ref[0, 0] = jnp.where(kv == 0, normed, out).astype(o_ref.dtype)


def _compress(top, bot, w1, w2, k_gain0):
    _, n_groups, n_half, k_half = top.shape
    hidden = w1.shape[2]
    blk = pl.BlockSpec((1, 1, n_half, k_half), lambda kv, g: (kv, g, 0, 0))
    return pl.pallas_call(
        _compress_kernel,
        grid=(2, n_groups),
        in_specs=[blk, blk,
                  pl.BlockSpec((1, 2 * k_half, hidden), lambda kv, g: (kv, 0, 0)),
                  pl.BlockSpec((1, hidden, HEAD_DIM), lambda kv, g: (kv, 0, 0)),
                  pl.BlockSpec((1, LANES), lambda kv, g: (0, 0))],
        out_specs=pl.BlockSpec((1, 1, n_half, HEAD_DIM), lambda kv, g: (kv, g, 0, 0)),
        out_shape=jax.ShapeDtypeStruct((2, n_groups, n_half, HEAD_DIM), BF16),
        compiler_params=_params(("arbitrary", "arbitrary")),
    )(top, bot, w1, w2, k_gain0.reshape(1, LANES))


def _cmp_select_kernel(q_ref, kc_ref, vc_ref, ov_ref, gate_ref, oc_ref, sel_ref, *, tq, heads_per_group,
                       n_blocks_pad):
    i = pl.program_id(1)
    kc, vc = kc_ref[0, 0], vc_ref[0, 0]
    gates = gate_ref[0]
    n_cmp = kc.shape[0]
    tpos = i * tq + lax.broadcasted_iota(jnp.int32, (tq, n_cmp), 0)
    cidx = lax.broadcasted_iota(jnp.int32, (tq, n_cmp), 1)
    valid = cidx * CMP_STRIDE + (CMP_BLOCK - 1) <= tpos
    p_sum = jnp.zeros((tq, n_cmp), F32)
    for hh in range(heads_per_group):
        sl = slice(hh * LANES, (hh + 1) * LANES)
        s = lax.dot_general(q_ref[:, sl], kc, NT_DIMS, preferred_element_type=F32)
        s = jnp.where(valid, s, NEG)
        e = jnp.where(valid, jnp.exp2(s - jnp.max(s, axis=-1, keepdims=True)), 0.0)
        p = e * (1.0 / jnp.maximum(jnp.sum(e, axis=-1, keepdims=True), TINY))
        oc_ref[:, sl] = gates[:, 3 * hh:3 * hh + 1] * jnp.dot(p.astype(BF16), vc, preferred_element_type=F32)
        p_sum = p_sum + p
    p_hi = p_sum.astype(BF16)
    p_lo = (p_sum - p_hi.astype(F32)).astype(BF16)
    imp = (lax.dot_general(ov_ref[...], p_hi, NT_DIMS, preferred_element_type=F32)
           + lax.dot_general(ov_ref[...], p_lo, NT_DIMS, preferred_element_type=F32))
    blk = lax.broadcasted_iota(jnp.int32, (n_blocks_pad, tq), 0).astype(F32)
    cur = ((i * tq + lax.broadcasted_iota(jnp.int32, (n_blocks_pad, tq), 1)) // SLC_BLOCK).astype(F32)
    forced = (blk == 0.0) | (blk == cur) | (blk == cur - 1.0)
    imp = jnp.where(forced, BIG, imp)
    imp = jnp.where(blk > cur, NEG, imp)
    sel = jnp.zeros((n_blocks_pad, tq), F32)
    for _ in range(SLC_TOPK):
        top = jnp.max(imp, axis=0, keepdims=True)
        first = jnp.min(jnp.where(imp == top, blk, float(n_blocks_pad)), axis=0, keepdims=True)
        hit = blk == first
        sel = jnp.where(hit, 1.0, sel)
        imp = jnp.where(hit, -jnp.inf, imp)
    sel_ref[0] = (sel.T - 1.0).astype(sel_ref.dtype)


def _cmp_select(qn, kcv, gates, n_groups, heads_per_group):
    s = qn.shape[0]
    tq = 512
    n_cmp = kcv.shape[2]
    n_blocks_pad = LANES
    assert s // SLC_BLOCK <= n_blocks_pad
    cmp_start = np.arange(n_cmp) * CMP_STRIDE
    slc_start = np.arange(n_blocks_pad) * SLC_BLOCK
    overlap_t = ((cmp_start[None, :] <= slc_start[:, None] + SLC_BLOCK - 1)
                 & (cmp_start[None, :] + CMP_BLOCK - 1 >= slc_start[:, None]))
    overlap_t = jnp.asarray(overlap_t.astype(np.float32), dtype=BF16)
    wq = heads_per_group * HEAD_DIM
    return pl.pallas_call(
        functools.partial(_cmp_select_kernel, tq=tq, heads_per_group=heads_per_group, n_blocks_pad=n_blocks_pad),
        grid=(n_groups, s // tq),
        in_specs=[pl.BlockSpec((tq, wq), lambda g, i: (i, g)),
                  pl.BlockSpec((1, 1, n_cmp, HEAD_DIM), lambda g, i: (0, g, 0, 0)),
                  pl.BlockSpec((1, 1, n_cmp, HEAD_DIM), lambda g, i: (1, g, 0, 0)),
                  pl.BlockSpec((n_blocks_pad, n_cmp), lambda g, i: (0, 0)),
                  pl.BlockSpec((1, tq, LANES), lambda g, i: (g, i, 0))],
        out_specs=[pl.BlockSpec((tq, wq), lambda g, i: (i, g)),
                   pl.BlockSpec((1, tq, n_blocks_pad), lambda g, i: (g, i, 0))],
        out_shape=[jax.ShapeDtypeStruct(qn.shape, F32),
                   jax.ShapeDtypeStruct((n_groups, s, n_blocks_pad), BF16)],
        compiler_params=_params(("arbitrary", "arbitrary")),
    )(qn, kcv, kcv, overlap_t, gates)


def _slc_attention(qn, sel, ksk, vs, prev, gates, heads_per_group):
    n_groups, s, _ = ksk.shape
    tq, tk = 256, 256
    wq = heads_per_group * HEAD_DIM
    in_specs = [
        pl.BlockSpec((tq, wq), lambda g, i: (i, g)),
        pl.BlockSpec((1, tq, LANES), lambda g, i: (g, i, 0)),
        pl.BlockSpec((1, s, 2 * LANES), lambda g, i: (g, 0, 0)),
        pl.BlockSpec((1, s, 2 * LANES), lambda g, i: (g, 0, 0)),
        pl.BlockSpec((tq, wq), lambda g, i: (i, g)),
        pl.BlockSpec((1, tq, LANES), lambda g, i: (g, i, 0)),
    ]
    out_spec = pl.BlockSpec((tq, wq), lambda g, i: (i, g))
    return _attn_call("slc", n_groups, s // tq, in_specs, out_spec, jax.ShapeDtypeStruct(qn.shape, F32),
                      2 * LANES, (qn, sel, ksk, vs, prev, gates), tq=tq, tk=tk, nrep=heads_per_group,
                      group_sizes=(8, 4, 1))


def _win_attention(qn, kw, vw, prev, gates, heads_per_group):
    n_groups, s, _ = kw.shape
    tq = tk = 256
    assert 2 * tk == WINDOW
    wq = heads_per_group * HEAD_DIM
    in_specs = [
        pl.BlockSpec((tq, wq), lambda g, i: (i, g)),
        pl.BlockSpec((1, s, LANES), lambda g, i: (g, 0, 0)),
        pl.BlockSpec((1, s, 2 * LANES), lambda g, i: (g, 0, 0)),
        pl.BlockSpec((tq, wq), lambda g, i: (i, g)),
        pl.BlockSpec((1, tq, LANES), lambda g, i: (g, i, 0)),
    ]
    out_spec = pl.BlockSpec((tq, wq), lambda g, i: (i, g))
    return _attn_call("win", n_groups, s // tq, in_specs, out_spec, jax.ShapeDtypeStruct(qn.shape, BF16),
                      LANES, (qn, kw, vw, prev, gates), tq=tq, tk=tk, nrep=heads_per_group)


def _rope_tables(positions, dim):
    inv = ROPE_THETA ** (-jnp.arange(0, dim, 2, dtype=F32) / dim)
    ang = positions.astype(F32)[:, None] * inv
    cos, sin = jnp.cos(ang), jnp.sin(ang)
    reps = LANES // dim
    cos_t = jnp.tile(jnp.concatenate([cos, cos], axis=-1), (1, reps))
    sin_t = jnp.tile(jnp.concatenate([-sin, sin], axis=-1), (1, reps))
    return cos_t, sin_t


def _pad_cols(w, n):
    return jnp.pad(w, ((0, 0), (0, n - w.shape[1])))


def _even_mixer(h, w_in, b_forget, gain_diff, gain_fox, lam_params, subln, tables, lam_init):
    d_model = h.shape[1]
    n_heads = d_model // (2 * HEAD_DIM)
    n_in = 6 * n_heads * HEAD_DIM + 1024
    proj = _matmul(h, _pad_cols(w_in, n_in).astype(BF16), bm=1024, bn=1024, bk=d_model)
    qa, ka, va, qf, kf, vf = _even_prep(proj, *tables, gain_diff, gain_fox, b_forget, n_heads)
    return [_diff_attention(qa, ka, va, lam_params, subln, lam_init), _fox_attention(qf, kf, vf)]


def _odd_mixer(h, w_in, q_gain, k_gain, cmp_pos, cmp_w1, cmp_w2, tables):
    s, d_model = h.shape
    n_heads, n_groups = d_model // HEAD_DIM, N_KV_GROUPS
    n_in = d_model + 7 * n_groups * HEAD_DIM
    proj = _matmul(h, _pad_cols(w_in, n_in).astype(BF16), bm=1024, bn=n_in // 10, bk=d_model)
    (qn, ktop, kbot, vtop, vbot, ksk, vs, kw, vw, gates) = _odd_prep(
        proj, *tables, q_gain, k_gain, cmp_pos, n_heads, n_groups)

    def halves(k_part, v_part):
        a = jnp.stack([k_part, v_part]).reshape(2, s // CMP_STRIDE, CMP_STRIDE, n_groups, HEAD_DIM)
        return jnp.transpose(a, (0, 3, 1, 2, 4)).reshape(2, n_groups, s // CMP_STRIDE, CMP_STRIDE * HEAD_DIM)

    kcv = _compress(halves(ktop, vtop), halves(kbot, vbot), cmp_w1, cmp_w2, k_gain[0])
    hpg = n_heads // n_groups
    o_cmp, sel = _cmp_select(qn, kcv, gates, n_groups, hpg)
    o_cmp_slc = _slc_attention(qn, sel, ksk, vs, o_cmp, gates, hpg)
    return [_win_attention(qn, kw, vw, o_cmp_slc, gates, hpg)]


def kernel(x, c, positions, w_ada, b_ada, norm_gain, even_w_in, even_b_forget, even_qk_gain_diff,
           even_qk_gain_fox, even_diff_lambda, even_diff_subln, even_w_out, odd_w_in, odd_q_gain,
           odd_k_gain, odd_cmp_pos, odd_cmp_w1, odd_cmp_w2, odd_w_out, mlp_w1, mlp_w2):
    batch, s, d_model = x.shape
    assert batch == 1
    depth = w_ada.shape[0]
    xs = x[0]
    pos = positions[0]
    tables64 = _rope_tables(pos, DIFF_QK_DIM)
    tables128 = _rope_tables(pos, HEAD_DIM)
    mod = _adaln(c, w_ada, b_ada)
    for i in range(depth):
        sh1, sc1, g1, sh2, sc2, g2 = [mod[i, :, k * d_model:(k + 1) * d_model] for k in range(6)]
        h = _norm_mod(xs, norm_gain[i, 0], sc1, sh1)
        if i % 2 == 0:
            e = i // 2
            lam_init = 0.8 - 0.6 * math.exp(-0.3 * i)
            o_parts = _even_mixer(h, even_w_in[e], even_b_forget[e], even_qk_gain_diff[e], even_qk_gain_fox[e],
                                  even_diff_lambda[e], even_diff_subln[e], tables64, lam_init)
            w_out, w_layer = even_w_out, e
        else:
            od = i // 2
            o_parts = _odd_mixer(h, odd_w_in[od], odd_q_gain[od], odd_k_gain[od], odd_cmp_pos[od],
                                 odd_cmp_w1[od], odd_cmp_w2[od], tables128)
            w_out, w_layer = odd_w_out, od
        xs = _matmul_resident(o_parts, w_out, w_layer, bm=1024, bn=512, epilogue="resid", out_dtype=F32,
                              resid=xs, gate=g1)
        h = _norm_mod(xs, norm_gain[i, 1], sc2, sh2)
        ff = _matmul_resident([h], mlp_w1, i, bm=512, bn=1024, epilogue="relu2", out_dtype=BF16)
        xs = _matmul(ff, mlp_w2[i].astype(BF16), bm=1024, bn=1024, bk=2048, epilogue="resid", resid=xs, gate=g2)
    return xs[None]
```

```python
import functools
import math

import numpy as np
import jax
import jax.numpy as jnp
from jax import lax
from jax.experimental import pallas as pl
from jax.experimental.pallas import tpu as pltpu

F32 = jnp.float32
BF16 = jnp.bfloat16

HEAD_DIM = 128
DIFF_QK_DIM = 64
N_KV_GROUPS = 4
CMP_BLOCK = 32
CMP_STRIDE = 16
SLC_BLOCK = 64
SLC_TOPK = 16
WINDOW = 512
ROPE_THETA = 10000.0
EPS = 1e-6
NEG = -1e30
BIG = 1e30
TINY = 1e-20
UNSELECTED_SCORE = 2.0 ** 99
LOG2E = 1.4426950408889634
LANES = 128
VMEM_LIMIT = 56 * 1024 * 1024

NT_DIMS = (((1,), (1,)), ((), ()))


def _params(sem):
    return pltpu.CompilerParams(dimension_semantics=sem, vmem_limit_bytes=VMEM_LIMIT)


def _adaln_kernel(cb_ref, w_ref, b_ref, o_ref):
    d_model, tn = w_ref.shape[1], w_ref.shape[2]
    rows_per_step = 256

    def body(r, acc):
        rows = pl.ds(pl.multiple_of(r * rows_per_step, rows_per_step), rows_per_step)
        cb = cb_ref[rows, :]
        cond = cb * jax.nn.sigmoid(cb)
        prod = w_ref[0, rows, :] * jnp.concatenate([cond] * (tn // LANES), axis=1)
        return acc + prod.reshape(rows_per_step // 8, 8, tn).sum(axis=0)

    acc = lax.fori_loop(0, d_model // rows_per_step, body, jnp.zeros((8, tn), F32))
    o_ref[0] = acc.sum(axis=0, keepdims=True) + b_ref[0]


def _adaln(c, w_ada, b_ada):
    depth, d_model, n_out = w_ada.shape
    tn = 512
    cb = jnp.broadcast_to(c.reshape(d_model, 1), (d_model, LANES))
    return pl.pallas_call(
        _adaln_kernel,
        grid=(depth, n_out // tn),
        in_specs=[
            pl.BlockSpec((d_model, LANES), lambda l, n: (0, 0)),
            pl.BlockSpec((1, d_model, tn), lambda l, n: (l, 0, n)),
            pl.BlockSpec((1, 1, tn), lambda l, n: (l, 0, n)),
        ],
        out_specs=pl.BlockSpec((1, 1, tn), lambda l, n: (l, 0, n)),
        out_shape=jax.ShapeDtypeStruct((depth, 1, n_out), F32),
        compiler_params=_params(("arbitrary", "arbitrary")),
    )(cb, w_ada, b_ada.reshape(depth, 1, n_out))


def _norm_mod_kernel(x_ref, g_ref, sc_ref, sh_ref, o_ref):
    x = x_ref[...]
    ms = jnp.mean(x * x, axis=-1, keepdims=True)
    y = x * lax.rsqrt(ms + EPS) * g_ref[...]
    o_ref[...] = (y * (1.0 + sc_ref[...]) + sh_ref[...]).astype(o_ref.dtype)


def _norm_mod(x, gain, scale, shift):
    s, d_model = x.shape
    ts = 256
    row = pl.BlockSpec((1, d_model), lambda i: (0, 0))
    return pl.pallas_call(
        _norm_mod_kernel,
        grid=(s // ts,),
        in_specs=[pl.BlockSpec((ts, d_model), lambda i: (i, 0)), row, row, row],
        out_specs=pl.BlockSpec((ts, d_model), lambda i: (i, 0)),
        out_shape=jax.ShapeDtypeStruct((s, d_model), BF16),
        compiler_params=_params(("arbitrary",)),
    )(x, gain.reshape(1, d_model), scale, shift)


def _mm_kernel(*refs, nk, epilogue):
    if epilogue == "resid":
        a_ref, b_ref, x_ref, g_ref, o_ref = refs[:5]
    else:
        a_ref, b_ref, o_ref = refs[:3]

    def finish(acc):
        if epilogue == "plain":
            o_ref[...] = acc.astype(o_ref.dtype)
        elif epilogue == "relu2":
            r = jnp.maximum(acc, 0.0)
            o_ref[...] = (r * r).astype(o_ref.dtype)
        else:
            o_ref[...] = x_ref[...] + g_ref[...] * acc

    if nk == 1:
        finish(jnp.dot(a_ref[...], b_ref[...], preferred_element_type=F32))
    else:
        acc_ref = refs[-1]
        k = pl.program_id(2)

        @pl.when(k == 0)
        def _():
            acc_ref[...] = jnp.zeros(acc_ref.shape, F32)

        acc_ref[...] += jnp.dot(a_ref[...], b_ref[...], preferred_element_type=F32)

        @pl.when(k == nk - 1)
        def _():
            finish(acc_ref[...])


def _matmul(a, b, *, bm, bn, bk, epilogue="plain", out_dtype=F32, resid=None, gate=None):
    m, kdim = a.shape
    n = b.shape[1]
    nk = kdim // bk
    in_specs = [pl.BlockSpec((bm, bk), lambda i, j, k: (i, k)),
                pl.BlockSpec((bk, bn), lambda i, j, k: (k, j))]
    args = [a, b]
    if epilogue == "resid":
        in_specs += [pl.BlockSpec((bm, bn), lambda i, j, k: (i, j)),
                     pl.BlockSpec((1, bn), lambda i, j, k: (0, j))]
        args += [resid, gate]
    scratch = [pltpu.VMEM((bm, bn), F32)] if nk > 1 else []
    return pl.pallas_call(
        functools.partial(_mm_kernel, nk=nk, epilogue=epilogue),
        grid=(m // bm, n // bn, nk),
        in_specs=in_specs,
        out_specs=pl.BlockSpec((bm, bn), lambda i, j, k: (i, j)),
        out_shape=jax.ShapeDtypeStruct((m, n), out_dtype),
        scratch_shapes=scratch,
        compiler_params=_params(("arbitrary", "arbitrary", "arbitrary")),
    )(*args)


def _mm_resident_kernel(*refs, n_a, epilogue):
    a_refs, b_ref = refs[:n_a], refs[n_a]
    if epilogue == "resid":
        x_ref, g_ref, o_ref, b_bf_ref = refs[n_a + 1:]
    else:
        o_ref, b_bf_ref = refs[n_a + 1:]

    @pl.when(pl.program_id(1) == 0)
    def _():
        b_bf_ref[...] = b_ref[0].astype(BF16)

    k_part = b_bf_ref.shape[0] // n_a
    acc = jnp.dot(a_refs[0][...], b_bf_ref[0:k_part, :], preferred_element_type=F32)
    for t in range(1, n_a):
        acc = acc + jnp.dot(a_refs[t][...], b_bf_ref[t * k_part:(t + 1) * k_part, :], preferred_element_type=F32)
    if epilogue == "relu2":
        r = jnp.maximum(acc, 0.0)
        o_ref[...] = (r * r).astype(o_ref.dtype)
    else:
        o_ref[...] = x_ref[...] + g_ref[...] * acc


def _matmul_resident(a_parts, w, layer, *, bm, bn, epilogue, out_dtype, resid=None, gate=None):
    m = a_parts[0].shape[0]
    _, kdim, n = w.shape
    n_a = len(a_parts)
    in_specs = [pl.BlockSpec((bm, kdim // n_a), lambda j, i: (i, 0)) for _ in a_parts]
    in_specs.append(pl.BlockSpec((1, kdim, bn), lambda j, i: (layer, 0, j)))
    args = list(a_parts) + [w]
    if epilogue == "resid":
        in_specs += [pl.BlockSpec((bm, bn), lambda j, i: (i, j)), pl.BlockSpec((1, bn), lambda j, i: (0, j))]
        args += [resid, gate]
    return pl.pallas_call(
        functools.partial(_mm_resident_kernel, n_a=n_a, epilogue=epilogue),
        grid=(n // bn, m // bm),
        in_specs=in_specs,
        out_specs=pl.BlockSpec((bm, bn), lambda j, i: (i, j)),
        out_shape=jax.ShapeDtypeStruct((m, n), out_dtype),
        scratch_shapes=[pltpu.VMEM((kdim, bn), BF16)],
        compiler_params=_params(("arbitrary", "arbitrary")),
    )(*args)


def _rms(x, n_chunk, gain):
    sq = x * x
    if n_chunk == LANES:
        ss = jnp.sum(sq, axis=-1, keepdims=True)
    else:
        lane = lax.broadcasted_iota(jnp.int32, x.shape, 1)
        low = lane < n_chunk
        s_low = jnp.sum(jnp.where(low, sq, 0.0), axis=-1, keepdims=True)
        s_all = jnp.sum(sq, axis=-1, keepdims=True)
        ss = jnp.where(low, s_low, s_all - s_low)
    return x * lax.rsqrt(ss * (1.0 / n_chunk) + EPS) * gain


def _rope(x, cos_t, sin_t, n_chunk):
    half = n_chunk // 2
    if n_chunk == LANES:
        partner = pltpu.roll(x, half, 1)
    else:
        lane = lax.broadcasted_iota(jnp.int32, x.shape, 1)
        first = (lane % n_chunk) < half
        partner = jnp.where(first, pltpu.roll(x, LANES - half, 1), pltpu.roll(x, half, 1))
    return x * cos_t + partner * sin_t


def _split3(v):
    hi = v.astype(BF16)
    r1 = v - hi.astype(F32)
    mid = r1.astype(BF16)
    lo = (r1 - mid.astype(F32)).astype(BF16)
    return hi, mid, lo


def _even_prep_kernel(qa_ref, ka_ref, va_ref, qb_ref, kb_ref, vb_ref, fb_ref, cos_ref, sin_ref,
                      gdq_ref, gdk_ref, gfq_ref, gfk_ref, bf_ref,
                      qa_o, ka_o, va_o, qf_o, kf_o, vf_o, carry_ref, *, n_heads):
    i = pl.program_id(0)
    ts = qa_ref.shape[0]
    cos_t, sin_t = cos_ref[...], sin_ref[...]
    lane = lax.broadcasted_iota(jnp.int32, (ts, LANES), 1)
    ones_aug = jnp.where(lane < 3, 1.0, 0.0).astype(BF16)
    ones = jnp.ones((ts, LANES), BF16)

    z = fb_ref[...] + bf_ref[...]
    logf = jnp.minimum(z, 0.0) - jnp.log(1.0 + jnp.exp(-jnp.abs(z)))
    r = lax.broadcasted_iota(jnp.int32, (ts, ts), 0)
    c = lax.broadcasted_iota(jnp.int32, (ts, ts), 1)
    tri = jnp.where(c <= r, 1.0, 0.0).astype(BF16)

    @pl.when(i == 0)
    def _():
        carry_ref[...] = jnp.zeros_like(carry_ref)

    f_cum = carry_ref[0:1, :]
    for part in _split3(logf):
        f_cum = f_cum + jnp.dot(tri, part, preferred_element_type=F32)
    carry_ref[...] = jnp.broadcast_to(f_cum[ts - 1:ts, :], carry_ref.shape)
    neg_hi, neg_mid, neg_lo = _split3(f_cum * (-LOG2E))
    prow = lax.broadcasted_iota(jnp.int32, (LANES, LANES), 0)
    pcol = lax.broadcasted_iota(jnp.int32, (LANES, LANES), 1)

    for h in range(n_heads):
        sl = slice(h * LANES, (h + 1) * LANES)
        qa = _rope(_rms(qa_ref[:, sl], DIFF_QK_DIM, gdq_ref[...]), cos_t, sin_t, DIFF_QK_DIM)
        qa_o[h] = (qa * (DIFF_QK_DIM ** -0.5 * LOG2E)).astype(BF16)
        ka = _rope(_rms(ka_ref[:, sl], DIFF_QK_DIM, gdk_ref[...]), cos_t, sin_t, DIFF_QK_DIM)
        ka_o[h] = ka.astype(BF16)
        va_o[h, :, 0:LANES] = va_ref[:, sl].astype(BF16)
        va_o[h, :, LANES:2 * LANES] = ones
        qb = _rms(qb_ref[:, sl], HEAD_DIM, gfq_ref[...]) * (HEAD_DIM ** -0.5 * LOG2E)
        qf_o[h, :, 0:LANES] = qb.astype(BF16)
        qf_o[h, :, LANES:2 * LANES] = ones_aug
        kb = _rms(kb_ref[:, sl], HEAD_DIM, gfk_ref[...])
        kf_o[h, :, 0:LANES] = kb.astype(BF16)
        aug = jnp.zeros((ts, LANES), F32)
        for lane_idx, part in enumerate((neg_hi, neg_mid, neg_lo)):
            place = jnp.where((prow == h) & (pcol == lane_idx), 1.0, 0.0).astype(BF16)
            aug = aug + jnp.dot(part, place, preferred_element_type=F32)
        kf_o[h, :, LANES:2 * LANES] = aug.astype(BF16)
        vf_o[h, :, 0:LANES] = vb_ref[:, sl].astype(BF16)
        vf_o[h, :, LANES:2 * LANES] = ones


def _even_prep(proj, cos_t, sin_t, gain_diff, gain_fox, b_forget, n_heads):
    s = proj.shape[0]
    ts = 256
    w = n_heads * HEAD_DIM
    seg = lambda k: pl.BlockSpec((ts, w), lambda i, k=k: (i, k))
    tab = pl.BlockSpec((ts, LANES), lambda i: (i, 0))
    row = pl.BlockSpec((1, LANES), lambda i: (0, 0))
    gdq = jnp.tile(gain_diff[0], 2).reshape(1, LANES)
    gdk = jnp.tile(gain_diff[1], 2).reshape(1, LANES)
    bfp = jnp.zeros((1, LANES), F32).at[0, :n_heads].set(b_forget)
    narrow = jax.ShapeDtypeStruct((n_heads, s, LANES), BF16)
    wide = jax.ShapeDtypeStruct((n_heads, s, 2 * LANES), BF16)
    narrow_spec = pl.BlockSpec((n_heads, ts, LANES), lambda i: (0, i, 0))
    wide_spec = pl.BlockSpec((n_heads, ts, 2 * LANES), lambda i: (0, i, 0))
    return pl.pallas_call(
        functools.partial(_even_prep_kernel, n_heads=n_heads),
        grid=(s // ts,),
        in_specs=[seg(0), seg(1), seg(2), seg(3), seg(4), seg(5),
                  pl.BlockSpec((ts, LANES), lambda i: (i, 6 * w // LANES)),
                  tab, tab, row, row, row, row, row],
        out_specs=[narrow_spec, narrow_spec, wide_spec, wide_spec, wide_spec, wide_spec],
        out_shape=[narrow, narrow, wide, wide, wide, wide],
        scratch_shapes=[pltpu.VMEM((8, LANES), F32)],
        compiler_params=_params(("arbitrary",)),
    )(proj, proj, proj, proj, proj, proj, proj, cos_t, sin_t, gdq, gdk,
      gain_fox[0].reshape(1, LANES), gain_fox[1].reshape(1, LANES), bfp)


MASK_DIAG, MASK_EDGE = 0, 1


def _attn_kernel(*refs, mode, tq, tk, nrep, group_sizes, lam_init):
    bias_ref, refs = refs[0], refs[1:]
    if mode == "diff":
        q_ref, k_ref, v_ref, lam_ref, sub_ref, o_ref, qs_ref, m_ref, acc_ref = refs
    elif mode == "fox":
        q_ref, k_ref, v_ref, o_ref, m_ref, acc_ref = refs
    elif mode == "slc":
        q_ref, sel_ref, k_ref, v_ref, prev_ref, gate_ref, o_ref, qs_ref, m_ref, acc_ref = refs
    else:
        q_ref, k_ref, v_ref, prev_ref, gate_ref, o_ref, qs_ref, m_ref, acc_ref = refs
    i = pl.program_id(1)
    rows = nrep * tq
    q_lo = i * tq
    blk = min(tq, tk)
    n_blk = tq // blk

    def row0(b, rep):
        return (b * nrep + rep) * blk

    for b in range(n_blk):
        qrows = slice(b * blk, (b + 1) * blk)
        if mode == "diff":
            q = q_ref[0, qrows, :]
            lane = lax.broadcasted_iota(jnp.int32, q.shape, 1)
            qs_ref[row0(b, 0):row0(b, 0) + blk, :] = jnp.where(lane < DIFF_QK_DIM, q, jnp.zeros_like(q))
            qs_ref[row0(b, 1):row0(b, 1) + blk, :] = jnp.where(lane >= DIFF_QK_DIM, q, jnp.zeros_like(q))
        elif mode in ("slc", "win"):
            for hh in range(nrep):
                dst = slice(row0(b, hh), row0(b, hh) + blk)
                qs_ref[dst, 0:LANES] = q_ref[qrows, hh * LANES:(hh + 1) * LANES]
                if mode == "slc":
                    qs_ref[dst, LANES:2 * LANES] = sel_ref[0, qrows, :]
    n_streams = m_ref.shape[0]

    def reset(streams):
        for st in streams:
            m_ref[st] = jnp.full((rows, LANES), NEG, F32)
            acc_ref[st] = jnp.zeros((rows, 2 * LANES), F32)

    fresh = mode == "win"

    def process(chunks):
        for c, masked, st, r0 in chunks:
            nrows = rows - r0
            k0 = pl.multiple_of(c * tk, tk)
            k = k_ref[0, pl.ds(k0, tk), :]
            v = v_ref[0, pl.ds(k0, tk), :]
            q_all = q_ref[0, r0:rows, :] if mode == "fox" else qs_ref[r0:rows, :]
            s = lax.dot_general(q_all, k, NT_DIMS, preferred_element_type=F32)
            if masked is not None:
                nb = nrep * blk
                top = s[0:nb] + bias_ref[masked]
                s = top if nb == nrows else jnp.concatenate([top, s[nb:nrows]], axis=0)
            tiles = [s[:, t * LANES:(t + 1) * LANES] for t in range(tk // LANES)]
            tile_max = tiles[0]
            for t in tiles[1:]:
                tile_max = jnp.maximum(tile_max, t)
            row_max = jnp.max(tile_max, axis=-1, keepdims=True)
            if fresh:
                m_new = jnp.broadcast_to(row_max, (nrows, LANES))
            else:
                m_prev = m_ref[st, r0:rows, :]
                m_new = jnp.maximum(m_prev, row_max)
                alpha = jnp.exp2(m_prev - m_new)
            p = jnp.concatenate([jnp.exp2(t - m_new) for t in tiles], axis=1).astype(BF16)
            pv = jnp.dot(p, v, preferred_element_type=F32)
            if fresh:
                acc_ref[st, r0:rows, :] = pv
            else:
                acc_ref[st, r0:rows, 0:LANES] = alpha * acc_ref[st, r0:rows, 0:LANES] + pv[:, 0:LANES]
                acc_ref[st, r0:rows, LANES:2 * LANES] = (alpha * acc_ref[st, r0:rows, LANES:2 * LANES]
                                                         + pv[:, LANES:2 * LANES])
            m_ref[st, r0:rows, :] = m_new

    if mode == "win":
        @pl.when(i >= 2)
        def _():
            process([(i - 2, MASK_EDGE, 0, 0), (i - 1, None, 1, 0), (i, MASK_DIAG, 2, 0)])

        @pl.when(i == 1)
        def _():
            reset([0])
            process([(i - 1, None, 1, 0), (i, MASK_DIAG, 2, 0)])

        @pl.when(i == 0)
        def _():
            reset([0, 1])
            process([(i, MASK_DIAG, 2, 0)])
    else:
        reset(range(n_streams))
        n_full = q_lo // tk
        start = 0
        for n in group_sizes:
            count = (n_full - start) // n

            def group_body(g, carry, n=n, start=start):
                process([(start + g * n + j, None, j, 0) for j in range(n)])
                return carry

            lax.fori_loop(0, count, group_body, 0)
            start = start + count * n
        process([(n_full + j, MASK_DIAG, j, row0(j, 0)) for j in range(n_blk)])

    m_all = m_ref[0]
    for st in range(1, n_streams):
        m_all = jnp.maximum(m_all, m_ref[st])
    num = jnp.zeros((rows, LANES), F32)
    den = jnp.zeros((rows, LANES), F32)
    for st in range(n_streams):
        w = jnp.exp2(m_ref[st] - m_all)
        num = num + w * acc_ref[st, :, 0:LANES]
        den = den + w * acc_ref[st, :, LANES:2 * LANES]
    o = num / den
    if mode == "diff":
        lp = lam_ref[...]
        lam = (jnp.exp(jnp.sum(lp[0:1] * lp[1:2], axis=-1, keepdims=True))
               - jnp.exp(jnp.sum(lp[2:3] * lp[3:4], axis=-1, keepdims=True)) + lam_init)
    for b in range(n_blk):
        qrows = slice(b * blk, (b + 1) * blk)
        if mode == "diff":
            od = o[row0(b, 0):row0(b, 0) + blk] - lam * o[row0(b, 1):row0(b, 1) + blk]
            ms = jnp.mean(od * od, axis=-1, keepdims=True)
            o_ref[qrows, :] = (od * lax.rsqrt(ms + EPS) * sub_ref[...] * (1.0 - lam_init)).astype(o_ref.dtype)
        elif mode == "fox":
            o_ref[qrows, :] = o[qrows].astype(o_ref.dtype)
        else:
            branch = 1 if mode == "slc" else 2
            gates = gate_ref[0, qrows, :]
            for hh in range(nrep):
                cols = slice(hh * LANES, (hh + 1) * LANES)
                gate = gates[:, 3 * hh + branch:3 * hh + branch + 1]
                o_ref[qrows, cols] = (prev_ref[qrows, cols]
                                      + gate * o[row0(b, hh):row0(b, hh) + blk]).astype(o_ref.dtype)


def _mask_bias(nrep, tk, with_edge):
    row = np.tile(np.arange(tk), nrep)[:, None]
    col = np.arange(tk)[None, :]
    diag = np.where(col <= row, 0.0, NEG).astype(np.float32)
    kinds = [diag, np.where(col <= row, NEG, 0.0).astype(np.float32)] if with_edge else [diag]
    return jnp.asarray(np.stack(kinds))


def _attn_call(mode, n_outer, n_q, in_specs, out_spec, out_shape, qs_width, args, *, tq, tk, nrep,
               group_sizes=(), lam_init=0.0):
    assert tq % tk == 0
    rows = nrep * tq
    n_streams = max(max(group_sizes, default=1), tq // tk, 3 if mode == "win" else 1)
    bias = _mask_bias(nrep, tk, with_edge=(mode == "win"))
    scratch = [] if qs_width is None else [pltpu.VMEM((rows, qs_width), BF16)]
    scratch += [pltpu.VMEM((n_streams, rows, LANES), F32), pltpu.VMEM((n_streams, rows, 2 * LANES), F32)]
    bias_spec = pl.BlockSpec(bias.shape, lambda h, i: (0, 0, 0))
    return pl.pallas_call(
        functools.partial(_attn_kernel, mode=mode, tq=tq, tk=tk, nrep=nrep, group_sizes=group_sizes,
                          lam_init=lam_init),
        grid=(n_outer, n_q), in_specs=[bias_spec] + in_specs, out_specs=out_spec, out_shape=out_shape,
        scratch_shapes=scratch,
        compiler_params=_params(("arbitrary", "arbitrary")),
    )(bias, *args)


def _diff_attention(qa, ka, va, lam_params, subln, lam_init):
    n_heads, s, _ = qa.shape
    tq, tk = 1024, 256
    in_specs = [
        pl.BlockSpec((1, tq, LANES), lambda h, i: (h, i, 0)),
        pl.BlockSpec((1, s, LANES), lambda h, i: (h, 0, 0)),
        pl.BlockSpec((1, s, 2 * LANES), lambda h, i: (h, 0, 0)),
        pl.BlockSpec((4, DIFF_QK_DIM), lambda h, i: (0, 0)),
        pl.BlockSpec((1, LANES), lambda h, i: (0, 0)),
    ]
    out_spec = pl.BlockSpec((tq, LANES), lambda h, i: (i, h))
    return _attn_call("diff", n_heads, s // tq, in_specs, out_spec,
                      jax.ShapeDtypeStruct((s, n_heads * HEAD_DIM), BF16), LANES,
                      (qa, ka, va, lam_params, subln.reshape(1, LANES)),
                      tq=tq, tk=tk, nrep=2, group_sizes=(4,), lam_init=lam_init)


def _fox_attention(qf, kf, vf):
    n_heads, s, _ = qf.shape
    tq, tk = 1024, 256
    in_specs = [
        pl.BlockSpec((1, tq, 2 * LANES), lambda h, i: (h, i, 0)),
        pl.BlockSpec((1, s, 2 * LANES), lambda h, i: (h, 0, 0)),
        pl.BlockSpec((1, s, 2 * LANES), lambda h, i: (h, 0, 0)),
    ]
    out_spec = pl.BlockSpec((tq, LANES), lambda h, i: (i, h))
    return _attn_call("fox", n_heads, s // tq, in_specs, out_spec,
                      jax.ShapeDtypeStruct((s, n_heads * HEAD_DIM), BF16), None,
                      (qf, kf, vf), tq=tq, tk=tk, nrep=1, group_sizes=(8, 4))


def _odd_prep_kernel(q_ref, kc_ref, vc_ref, ks_ref, vs_ref, kw_ref, vw_ref, gl_ref, cos_ref, sin_ref,
                     qg_ref, kg_ref, pe_ref,
                     qn_o, ktop_o, kbot_o, vtop_o, vbot_o, ksk_o, vs_o, kw_o, vw_o, gates_o,
                     *, n_heads, n_groups):
    i = pl.program_id(0)
    ts = q_ref.shape[0]
    cos_t, sin_t = cos_ref[...], sin_ref[...]
    ones = jnp.ones((ts, LANES), BF16)
    for h in range(n_heads):
        sl = slice(h * LANES, (h + 1) * LANES)
        q = _rope(_rms(q_ref[:, sl], HEAD_DIM, qg_ref[...]), cos_t, sin_t, HEAD_DIM)
        qn_o[:, sl] = (q * (HEAD_DIM ** -0.5 * LOG2E)).astype(BF16)
    key_block = (i * ts + lax.broadcasted_iota(jnp.int32, (ts, LANES), 0)) // SLC_BLOCK
    lane = lax.broadcasted_iota(jnp.int32, (ts, LANES), 1)
    block_onehot = jnp.where(lane == key_block, UNSELECTED_SCORE, 0.0).astype(BF16)
    for g in range(n_groups):
        sl = slice(g * LANES, (g + 1) * LANES)
        kc = _rope(kc_ref[:, sl], cos_t, sin_t, HEAD_DIM)
        ktop_o[:, sl] = (kc + pe_ref[0]).astype(BF16)
        kbot_o[:, sl] = (kc + pe_ref[1]).astype(BF16)
        vc = vc_ref[:, sl]
        vtop_o[:, sl] = (vc + pe_ref[2]).astype(BF16)
        vbot_o[:, sl] = (vc + pe_ref[3]).astype(BF16)
        ks = _rope(_rms(ks_ref[:, sl], HEAD_DIM, kg_ref[1:2, :]), cos_t, sin_t, HEAD_DIM)
        ksk_o[g, :, 0:LANES] = ks.astype(BF16)
        ksk_o[g, :, LANES:2 * LANES] = block_onehot
        kw = _rope(_rms(kw_ref[:, sl], HEAD_DIM, kg_ref[2:3, :]), cos_t, sin_t, HEAD_DIM)
        kw_o[g] = kw.astype(BF16)
        vs_o[g, :, 0:LANES] = vs_ref[:, sl].astype(BF16)
        vs_o[g, :, LANES:2 * LANES] = ones
        vw_o[g, :, 0:LANES] = vw_ref[:, sl].astype(BF16)
        vw_o[g, :, LANES:2 * LANES] = ones
    gates = jax.nn.sigmoid(gl_ref[...])
    lanes_per_group = 3 * n_heads // n_groups
    for g in range(n_groups):
        gates_o[g] = gates if g == 0 else pltpu.roll(gates, LANES - g * lanes_per_group, 1)


def _odd_prep(proj, cos_t, sin_t, q_gain, k_gain, cmp_pos, n_heads, n_groups):
    s = proj.shape[0]
    ts = 256
    wq, wg = n_heads * HEAD_DIM, n_groups * HEAD_DIM
    seg = lambda k: pl.BlockSpec((ts, wg), lambda i, k=k: (i, wq // wg + k))
    tab = pl.BlockSpec((ts, LANES), lambda i: (i, 0))
    grp = pl.BlockSpec((ts, wg), lambda i: (i, 0))
    narrow_spec = pl.BlockSpec((n_groups, ts, LANES), lambda i: (0, i, 0))
    wide_spec = pl.BlockSpec((n_groups, ts, 2 * LANES), lambda i: (0, i, 0))
    pe_tiles = jnp.stack([jnp.tile(cmp_pos[kv, half * CMP_STRIDE:(half + 1) * CMP_STRIDE], (ts // CMP_STRIDE, 1))
                          for kv in range(2) for half in range(2)])
    g_bf = jax.ShapeDtypeStruct((s, wg), BF16)
    narrow = jax.ShapeDtypeStruct((n_groups, s, LANES), BF16)
    wide = jax.ShapeDtypeStruct((n_groups, s, 2 * LANES), BF16)
    return pl.pallas_call(
        functools.partial(_odd_prep_kernel, n_heads=n_heads, n_groups=n_groups),
        grid=(s // ts,),
        in_specs=[pl.BlockSpec((ts, wq), lambda i: (i, 0)), seg(0), seg(1), seg(2), seg(3), seg(4), seg(5),
                  pl.BlockSpec((ts, LANES), lambda i: (i, (wq + 6 * wg) // LANES)),
                  tab, tab,
                  pl.BlockSpec((1, LANES), lambda i: (0, 0)),
                  pl.BlockSpec((3, LANES), lambda i: (0, 0)),
                  pl.BlockSpec((4, ts, LANES), lambda i: (0, 0, 0))],
        out_specs=[pl.BlockSpec((ts, wq), lambda i: (i, 0)), grp, grp, grp, grp,
                   wide_spec, wide_spec, narrow_spec, wide_spec, narrow_spec],
        out_shape=[jax.ShapeDtypeStruct((s, wq), BF16), g_bf, g_bf, g_bf, g_bf,
                   wide, wide, narrow, wide, jax.ShapeDtypeStruct((n_groups, s, LANES), F32)],
        compiler_params=_params(("arbitrary",)),
    )(proj, proj, proj, proj, proj, proj, proj, proj, cos_t, sin_t,
      q_gain.reshape(1, LANES), k_gain, pe_tiles)


def _compress_kernel(top_ref, bot_ref, w1_ref, w2_ref, kg_ref, o_ref):
    kv = pl.program_id(0)
    n_half = top_ref.shape[2]
    k_half = top_ref.shape[3]
    h_top = jnp.dot(top_ref[0, 0], w1_ref[0, 0:k_half, :].astype(BF16), preferred_element_type=F32)
    h_bot = jnp.dot(bot_ref[0, 0], w1_ref[0, k_half:2 * k_half, :].astype(BF16), preferred_element_type=F32)
    hid = h_top + pltpu.roll(h_bot, n_half - 1, 0)
    act = hid * jax.nn.sigmoid(hid)
    out = jnp.dot(act.astype(BF16), w2_ref[0].astype(BF16), preferred_element_type=F32)
    ms = jnp.mean(out * out, axis=-1, keepdims=True)
    normed = out * lax.rsqrt(ms + EPS) * kg_ref[...]
    o_ref[0, 0] = jnp.where(kv == 0, normed, out).astype(o_ref.dtype)


def _compress(top, bot, w1, w2, k_gain0):
    _, n_groups, n_half, k_half = top.shape
    hidden = w1.shape[2]
    blk = pl.BlockSpec((1, 1, n_half, k_half), lambda kv, g: (kv, g, 0, 0))
    return pl.pallas_call(
        _compress_kernel,
        grid=(2, n_groups),
        in_specs=[blk, blk,
                  pl.BlockSpec((1, 2 * k_half, hidden), lambda kv, g: (kv, 0, 0)),
                  pl.BlockSpec((1, hidden, HEAD_DIM), lambda kv, g: (kv, 0, 0)),
                  pl.BlockSpec((1, LANES), lambda kv, g: (0, 0))],
        out_specs=pl.BlockSpec((1, 1, n_half, HEAD_DIM), lambda kv, g: (kv, g, 0, 0)),
        out_shape=jax.ShapeDtypeStruct((2, n_groups, n_half, HEAD_DIM), BF16),
        compiler_params=_params(("arbitrary", "arbitrary")),
    )(top, bot, w1, w2, k_gain0.reshape(1, LANES))


def _cmp_select_kernel(q_ref, kc_ref, vc_ref, ov_ref, gate_ref, oc_ref, sel_ref, *, tq, heads_per_group,
                       n_blocks_pad):
    i = pl.program_id(1)
    kc, vc = kc_ref[0, 0], vc_ref[0, 0]
    gates = gate_ref[0]
    n_cmp = kc.shape[0]
    tpos = i * tq + lax.broadcasted_iota(jnp.int32, (tq, n_cmp), 0)
    cidx = lax.broadcasted_iota(jnp.int32, (tq, n_cmp), 1)
    valid = cidx * CMP_STRIDE + (CMP_BLOCK - 1) <= tpos
    p_sum = jnp.zeros((tq, n_cmp), F32)
    for hh in range(heads_per_group):
        sl = slice(hh * LANES, (hh + 1) * LANES)
        s = lax.dot_general(q_ref[:, sl], kc, NT_DIMS, preferred_element_type=F32)
        s = jnp.where(valid, s, NEG)
        e = jnp.where(valid, jnp.exp2(s - jnp.max(s, axis=-1, keepdims=True)), 0.0)
        p = e * (1.0 / jnp.maximum(jnp.sum(e, axis=-1, keepdims=True), TINY))
        oc_ref[:, sl] = gates[:, 3 * hh:3 * hh + 1] * jnp.dot(p.astype(BF16), vc, preferred_element_type=F32)
        p_sum = p_sum + p
    p_hi = p_sum.astype(BF16)
    p_lo = (p_sum - p_hi.astype(F32)).astype(BF16)
    imp = (lax.dot_general(ov_ref[...], p_hi, NT_DIMS, preferred_element_type=F32)
           + lax.dot_general(ov_ref[...], p_lo, NT_DIMS, preferred_element_type=F32))
    blk = lax.broadcasted_iota(jnp.int32, (n_blocks_pad, tq), 0).astype(F32)
    cur = ((i * tq + lax.broadcasted_iota(jnp.int32, (n_blocks_pad, tq), 1)) // SLC_BLOCK).astype(F32)
    forced = (blk == 0.0) | (blk == cur) | (blk == cur - 1.0)
    imp = jnp.where(forced, BIG, imp)
    imp = jnp.where(blk > cur, NEG, imp)
    sel = jnp.zeros((n_blocks_pad, tq), F32)
    for _ in range(SLC_TOPK):
        top = jnp.max(imp, axis=0, keepdims=True)
        first = jnp.min(jnp.where(imp == top, blk, float(n_blocks_pad)), axis=0, keepdims=True)
        hit = blk == first
        sel = jnp.where(hit, 1.0, sel)
        imp = jnp.where(hit, -jnp.inf, imp)
    sel_ref[0] = (sel.T - 1.0).astype(sel_ref.dtype)


def _cmp_select(qn, kcv, gates, n_groups, heads_per_group):
    s = qn.shape[0]
    tq = 512
    n_cmp = kcv.shape[2]
    n_blocks_pad = LANES
    assert s // SLC_BLOCK <= n_blocks_pad
    cmp_start = np.arange(n_cmp) * CMP_STRIDE
    slc_start = np.arange(n_blocks_pad) * SLC_BLOCK
    overlap_t = ((cmp_start[None, :] <= slc_start[:, None] + SLC_BLOCK - 1)
                 & (cmp_start[None, :] + CMP_BLOCK - 1 >= slc_start[:, None]))
    overlap_t = jnp.asarray(overlap_t.astype(np.float32), dtype=BF16)
    wq = heads_per_group * HEAD_DIM
    return pl.pallas_call(
        functools.partial(_cmp_select_kernel, tq=tq, heads_per_group=heads_per_group, n_blocks_pad=n_blocks_pad),
        grid=(n_groups, s // tq),
        in_specs=[pl.BlockSpec((tq, wq), lambda g, i: (i, g)),
                  pl.BlockSpec((1, 1, n_cmp, HEAD_DIM), lambda g, i: (0, g, 0, 0)),
                  pl.BlockSpec((1, 1, n_cmp, HEAD_DIM), lambda g, i: (1, g, 0, 0)),
                  pl.BlockSpec((n_blocks_pad, n_cmp), lambda g, i: (0, 0)),
                  pl.BlockSpec((1, tq, LANES), lambda g, i: (g, i, 0))],
        out_specs=[pl.BlockSpec((tq, wq), lambda g, i: (i, g)),
                   pl.BlockSpec((1, tq, n_blocks_pad), lambda g, i: (g, i, 0))],
        out_shape=[jax.ShapeDtypeStruct(qn.shape, F32),
                   jax.ShapeDtypeStruct((n_groups, s, n_blocks_pad), BF16)],
        compiler_params=_params(("arbitrary", "arbitrary")),
    )(qn, kcv, kcv, overlap_t, gates)


def _slc_attention(qn, sel, ksk, vs, prev, gates, heads_per_group):
    n_groups, s, _ = ksk.shape
    tq, tk = 256, 256
    wq = heads_per_group * HEAD_DIM
    in_specs = [
        pl.BlockSpec((tq, wq), lambda g, i: (i, g)),
        pl.BlockSpec((1, tq, LANES), lambda g, i: (g, i, 0)),
        pl.BlockSpec((1, s, 2 * LANES), lambda g, i: (g, 0, 0)),
        pl.BlockSpec((1, s, 2 * LANES), lambda g, i: (g, 0, 0)),
        pl.BlockSpec((tq, wq), lambda g, i: (i, g)),
        pl.BlockSpec((1, tq, LANES), lambda g, i: (g, i, 0)),
    ]
    out_spec = pl.BlockSpec((tq, wq), lambda g, i: (i, g))
    return _attn_call("slc", n_groups, s // tq, in_specs, out_spec, jax.ShapeDtypeStruct(qn.shape, F32),
                      2 * LANES, (qn, sel, ksk, vs, prev, gates), tq=tq, tk=tk, nrep=heads_per_group,
                      group_sizes=(4, 1))


def _win_attention(qn, kw, vw, prev, gates, heads_per_group):
    n_groups, s, _ = kw.shape
    tq = tk = 256
    assert 2 * tk == WINDOW
    wq = heads_per_group * HEAD_DIM
    in_specs = [
        pl.BlockSpec((tq, wq), lambda g, i: (i, g)),
        pl.BlockSpec((1, s, LANES), lambda g, i: (g, 0, 0)),
        pl.BlockSpec((1, s, 2 * LANES), lambda g, i: (g, 0, 0)),
        pl.BlockSpec((tq, wq), lambda g, i: (i, g)),
        pl.BlockSpec((1, tq, LANES), lambda g, i: (g, i, 0)),
    ]
    out_spec = pl.BlockSpec((tq, wq), lambda g, i: (i, g))
    return _attn_call("win", n_groups, s // tq, in_specs, out_spec, jax.ShapeDtypeStruct(qn.shape, BF16),
                      LANES, (qn, kw, vw, prev, gates), tq=tq, tk=tk, nrep=heads_per_group)


def _rope_tables(positions, dim):
    inv = ROPE_THETA ** (-jnp.arange(0, dim, 2, dtype=F32) / dim)
    ang = positions.astype(F32)[:, None] * inv
    cos, sin = jnp.cos(ang), jnp.sin(ang)
    reps = LANES // dim
    cos_t = jnp.tile(jnp.concatenate([cos, cos], axis=-1), (1, reps))
    sin_t = jnp.tile(jnp.concatenate([-sin, sin], axis=-1), (1, reps))
    return cos_t, sin_t


def _pad_cols(w, n):
    return jnp.pad(w, ((0, 0), (0, n - w.shape[1])))


def _even_mixer(h, w_in, b_forget, gain_diff, gain_fox, lam_params, subln, tables, lam_init):
    d_model = h.shape[1]
    n_heads = d_model // (2 * HEAD_DIM)
    n_in = 6 * n_heads * HEAD_DIM + 1024
    proj = _matmul(h, _pad_cols(w_in, n_in).astype(BF16), bm=1024, bn=1024, bk=d_model)
    qa, ka, va, qf, kf, vf = _even_prep(proj, *tables, gain_diff, gain_fox, b_forget, n_heads)
    return [_diff_attention(qa, ka, va, lam_params, subln, lam_init), _fox_attention(qf, kf, vf)]


def _odd_mixer(h, w_in, q_gain, k_gain, cmp_pos, cmp_w1, cmp_w2, tables):
    s, d_model = h.shape
    n_heads, n_groups = d_model // HEAD_DIM, N_KV_GROUPS
    n_in = d_model + 7 * n_groups * HEAD_DIM
    proj = _matmul(h, _pad_cols(w_in, n_in).astype(BF16), bm=1024, bn=n_in // 10, bk=d_model)
    (qn, ktop, kbot, vtop, vbot, ksk, vs, kw, vw, gates) = _odd_prep(
        proj, *tables, q_gain, k_gain, cmp_pos, n_heads, n_groups)

    def halves(k_part, v_part):
        a = jnp.stack([k_part, v_part]).reshape(2, s // CMP_STRIDE, CMP_STRIDE, n_groups, HEAD_DIM)
        return jnp.transpose(a, (0, 3, 1, 2, 4)).reshape(2, n_groups, s // CMP_STRIDE, CMP_STRIDE * HEAD_DIM)

    kcv = _compress(halves(ktop, vtop), halves(kbot, vbot), cmp_w1, cmp_w2, k_gain[0])
    hpg = n_heads // n_groups
    o_cmp, sel = _cmp_select(qn, kcv, gates, n_groups, hpg)
    o_cmp_slc = _slc_attention(qn, sel, ksk, vs, o_cmp, gates, hpg)
    return [_win_attention(qn, kw, vw, o_cmp_slc, gates, hpg)]


def kernel(x, c, positions, w_ada, b_ada, norm_gain, even_w_in, even_b_forget, even_qk_gain_diff,
           even_qk_gain_fox, even_diff_lambda, even_diff_subln, even_w_out, odd_w_in, odd_q_gain,
           odd_k_gain, odd_cmp_pos, odd_cmp_w1, odd_cmp_w2, odd_w_out, mlp_w1, mlp_w2):
    batch, s, d_model = x.shape
    assert batch == 1
    depth = w_ada.shape[0]
    xs = x[0]
    pos = positions[0]
    tables64 = _rope_tables(pos, DIFF_QK_DIM)
    tables128 = _rope_tables(pos, HEAD_DIM)
    mod = _adaln(c, w_ada, b_ada)
    for i in range(depth):
        sh1, sc1, g1, sh2, sc2, g2 = [mod[i, :, k * d_model:(k + 1) * d_model] for k in range(6)]
        h = _norm_mod(xs, norm_gain[i, 0], sc1, sh1)
        if i % 2 == 0:
            e = i // 2
            lam_init = 0.8 - 0.6 * math.exp(-0.3 * i)
            o_parts = _even_mixer(h, even_w_in[e], even_b_forget[e], even_qk_gain_diff[e], even_qk_gain_fox[e],
                                  even_diff_lambda[e], even_diff_subln[e], tables64, lam_init)
            w_out, w_layer = even_w_out, e
        else:
            od = i // 2
            o_parts = _odd_mixer(h, odd_w_in[od], odd_q_gain[od], odd_k_gain[od], odd_cmp_pos[od],
                                 odd_cmp_w1[od], odd_cmp_w2[od], tables128)
            w_out, w_layer = odd_w_out, od
        xs = _matmul_resident(o_parts, w_out, w_layer, bm=1024, bn=512, epilogue="resid", out_dtype=F32,
                              resid=xs, gate=g1)
        h = _norm_mod(xs, norm_gain[i, 1], sc2, sh2)
        ff = _matmul_resident([h], mlp_w1, i, bm=512, bn=1024, epilogue="relu2", out_dtype=BF16)
        xs = _matmul(ff, mlp_w2[i].astype(BF16), bm=1024, bn=1024, bk=2048, epilogue="resid", resid=xs, gate=g2)
    return xs[None]
```
